```python
import jax, jax.numpy as jnp
from jax import lax
import numpy as np

D_MODEL = 1024
BATCH = 8
SEQ = 2048
DEPTH = 4
DEC_BATCH = 128
DEC_SEQ = 8
PAST_LEN = 8192
PAGE_SIZE = 128

N_MEM = 256
HEAD_DIM = 64
N_NSA_HEADS = 8
N_FOX_HEADS = 8
N_MLA_HEADS = 8
MLA_NOPE = 64
MLA_ROPE = 32
MLA_QK = MLA_NOPE + MLA_ROPE
MLA_V = 64
KV_LORA = 256
N_BRANCH = 3
BRANCH_W = 512
L_CMP = 32
STRIDE_CMP = 16
L_SEL = 64
N_SELECT = 16
WINDOW = 512
N_MEM_HEADS = 4
MEM_HD = 128
MEM_W = N_MEM_HEADS * MEM_HD
D_FF = 4 * D_MODEL
ROPE_THETA = 10000.0
EPS = 1e-6
Q_BLOCK = 128
NSA_Q = N_NSA_HEADS * HEAD_DIM
NSA_KV = 6 * HEAD_DIM
NSA_G = 3 * N_NSA_HEADS
FOX_Q = N_FOX_HEADS * HEAD_DIM
FOX_KV = 2 * HEAD_DIM
FOX_F = N_FOX_HEADS
MLA_Q = N_MLA_HEADS * MLA_QK
MERGE_G = N_BRANCH * D_MODEL
IN_SPLITS = (NSA_Q, NSA_KV, NSA_G, FOX_Q, FOX_KV, FOX_F, MLA_Q, KV_LORA, MLA_ROPE, MERGE_G)
N_IN = NSA_Q + NSA_KV + NSA_G + FOX_Q + FOX_KV + FOX_F + MLA_Q + KV_LORA + MLA_ROPE + MERGE_G

kernel_name = 'hybrid_nsa_fox_mla_step'


def split_offsets():
    return [int(o) for o in np.cumsum(np.array(IN_SPLITS))[:-1]]


def rmsnorm(x, g):
    xf = x.astype(jnp.float32)
    y = xf * lax.rsqrt(jnp.mean(xf * xf, axis=-1, keepdims=True) + EPS)
    return (y * g.astype(jnp.float32)).astype(x.dtype)


def rope(x, pos):
    half = x.shape[-1] // 2
    inv = ROPE_THETA ** (-jnp.arange(half, dtype=jnp.float32) / half)
    ang = pos.astype(jnp.float32)[:, None] * inv[None, :]
    cos = jnp.cos(ang)[:, None, :]
    sin = jnp.sin(ang)[:, None, :]
    xf = x.astype(jnp.float32)
    x1, x2 = xf[..., :half], xf[..., half:]
    return jnp.concatenate([x1 * cos - x2 * sin, x2 * cos + x1 * sin], axis=-1).astype(x.dtype)


def masked_softmax(s, mask):
    s = jnp.where(mask, s.astype(jnp.float32), -1e30)
    m = jnp.max(s, axis=-1, keepdims=True)
    p = jnp.where(mask, jnp.exp(s - m), 0.0)
    return p / jnp.maximum(jnp.sum(p, axis=-1, keepdims=True), 1e-30)


def sweep_blocks(fn, T, *arrs):
    blk = T if T <= Q_BLOCK else Q_BLOCK
    nb = T // blk

    def to_blocks(a):
        return jnp.moveaxis(a.reshape(a.shape[0], nb, blk, *a.shape[2:]), 1, 0)

    starts = jnp.arange(nb, dtype=jnp.int32) * blk
    out = lax.map(lambda xs: fn(*xs), (starts,) + tuple(to_blocks(a) for a in arrs))
    out = jnp.moveaxis(out, 0, 1)
    return out.reshape(out.shape[0], T, *out.shape[3:])


def gather_pages(pool, page_table):
    g = pool[page_table]
    return g.reshape(g.shape[0], g.shape[1] * g.shape[2], g.shape[3])


def cmp_to_sel_weights(n_cmp, n_sel):
    cs = jnp.arange(n_cmp) * STRIDE_CMP
    ss = jnp.arange(n_sel) * L_SEL
    lo = jnp.maximum(cs[:, None], ss[None, :])
    hi = jnp.minimum(cs[:, None] + L_CMP, ss[None, :] + L_SEL)
    return (jnp.maximum(hi - lo, 0) // STRIDE_CMP).astype(jnp.float32)


def nsa_attend(q, gates, nsa_all, win_all, wb, g_kcmp, pe, w_cmp):
    B, T, H, _ = q.shape
    S = nsa_all.shape[1]
    P = S - T
    scale = HEAD_DIM ** -0.5
    n_cmp = (S - L_CMP) // STRIDE_CMP + 1
    idx = jnp.arange(n_cmp)[:, None] * STRIDE_CMP + jnp.arange(L_CMP)[None, :]
    cmp_rows = nsa_all[:, idx, :2 * HEAD_DIM]
    kc = jnp.einsum('bnld,lde->bne', cmp_rows[..., :HEAD_DIM] + pe[0], w_cmp[0])
    vc = jnp.einsum('bnld,lde->bne', cmp_rows[..., HEAD_DIM:] + pe[1], w_cmp[1])
    kc = rmsnorm(kc, g_kcmp)
    cmp_end = jnp.arange(n_cmp) * STRIDE_CMP + (L_CMP - 1)
    n_sel = -(-S // L_SEL)
    sel_w = cmp_to_sel_weights(n_cmp, n_sel)
    k_top = min(N_SELECT, n_sel)
    slc = jnp.pad(nsa_all[..., 2 * HEAD_DIM:], ((0, 0), (0, n_sel * L_SEL - S), (0, 0)))
    slc_blocks = slc.reshape(B, n_sel, L_SEL, 2 * HEAD_DIM)
    sel_ids = jnp.arange(n_sel)
    bidx = jnp.arange(B)[:, None, None]
    win_pad = jnp.pad(win_all, ((0, 0), (WINDOW, 0), (0, 0)))

    def block(start, qb, gb):
        blk = qb.shape[1]
        qpos = P + start + jnp.arange(blk)
        s_c = jnp.einsum('bqhd,bnd->bqhn', qb, kc) * scale
        p_c = masked_softmax(s_c, (cmp_end[None, :] <= qpos[:, None])[None, :, None, :])
        o_c = jnp.einsum('bqhn,bnd->bqhd', p_c.astype(vc.dtype), vc)
        imp = jnp.einsum('bqhn,nj->bqj', p_c, sel_w)
        cur = (qpos // L_SEL)[:, None]
        forced = (sel_ids[None, :] == 0) | (sel_ids[None, :] == cur) | (sel_ids[None, :] == cur - 1)
        avail = sel_ids[None, :] * L_SEL <= qpos[:, None]
        score = jnp.where(forced[None], jnp.inf, jnp.where(avail[None], imp, -jnp.inf))
        vals, sel = lax.top_k(score, k_top)
        kv_s = slc_blocks[bidx, sel].reshape(B, blk, k_top * L_SEL, 2 * HEAD_DIM)
        spos = (sel[..., None] * L_SEL + jnp.arange(L_SEL)).reshape(B, blk, k_top * L_SEL)
        mask_s = (spos <= qpos[None, :, None]) & jnp.repeat(vals > -jnp.inf, L_SEL, axis=-1)
        q_rot = rope(qb, qpos)
        s_s = jnp.einsum('bqhd,bqkd->bqhk', q_rot, kv_s[..., :HEAD_DIM]) * scale
        p_s = masked_softmax(s_s, mask_s[:, :, None, :])
        o_s = jnp.einsum('bqhk,bqkd->bqhd', p_s.astype(kv_s.dtype), kv_s[..., HEAD_DIM:])
        kv_w = lax.dynamic_slice_in_dim(win_pad, wb + start, WINDOW + blk, axis=1)
        m = jnp.arange(WINDOW + blk)[None, :]
        i = jnp.arange(blk)[:, None]
        mask_w = (m > i) & (m <= i + WINDOW) & (wb + start + m >= WINDOW)
        s_w = jnp.einsum('bqhd,bkd->bqhk', q_rot, kv_w[..., :HEAD_DIM]) * scale
        p_w = masked_softmax(s_w, mask_w[None, :, None, :])
        o_w = jnp.einsum('bqhk,bkd->bqhd', p_w.astype(kv_w.dtype), kv_w[..., HEAD_DIM:])
        return gb[..., 0:1] * o_c + gb[..., 1:2] * o_s + gb[..., 2:3] * o_w

    return sweep_blocks(block, T, q, gates)


def fox_attend(q, k_all, v_all, logf_all):
    B, T, H, _ = q.shape
    S = k_all.shape[1]
    P = S - T
    c = jnp.moveaxis(jnp.cumsum(logf_all.astype(jnp.float32), axis=1), 1, 2)
    cq = jnp.moveaxis(c[:, :, P:], 2, 1)
    kpos = jnp.arange(S)
    scale = HEAD_DIM ** -0.5

    def block(start, qb, cqb):
        blk = qb.shape[1]
        qpos = P + start + jnp.arange(blk)
        s = jnp.einsum('bqhd,bsd->bhqs', qb, k_all).astype(jnp.float32) * scale
        s = s + jnp.moveaxis(cqb, 1, 2)[..., None] - c[:, :, None, :]
        p = masked_softmax(s, (kpos[None, :] <= qpos[:, None])[None, None])
        return jnp.einsum('bhqs,bsd->bqhd', p.astype(v_all.dtype), v_all)

    return sweep_blocks(block, T, q, cq)


def mla_attend(q, ckv_all, kr_all, w_uk, w_uv, g_k):
    B, T, H, _ = q.shape
    S = ckv_all.shape[1]
    P = S - T
    k_nope = jnp.einsum('bsc,chd->bshd', ckv_all, w_uk)
    ms = (jnp.sum(jnp.square(k_nope.astype(jnp.float32)), axis=-1)
          + jnp.sum(jnp.square(kr_all.astype(jnp.float32)), axis=-1)[..., None]) / MLA_QK
    inv_r = jnp.moveaxis(lax.rsqrt(ms + EPS), 1, 2)[:, :, None, :]
    kn = k_nope * g_k[:MLA_NOPE]
    kr = kr_all * g_k[MLA_NOPE:]
    kpos = jnp.arange(S)
    scale = MLA_QK ** -0.5

    def block(start, qb):
        blk = qb.shape[1]
        qpos = P + start + jnp.arange(blk)
        s = (jnp.einsum('bqhd,bshd->bhqs', qb[..., :MLA_NOPE], kn)
             + jnp.einsum('bqhd,bsd->bhqs', qb[..., MLA_NOPE:], kr)).astype(jnp.float32)
        s = s * inv_r * scale
        p = masked_softmax(s, (kpos[None, :] <= qpos[:, None])[None, None])
        lat = jnp.einsum('bhqs,bsc->bqhc', p.astype(ckv_all.dtype), ckv_all)
        return jnp.einsum('bqhc,chd->bqhd', lat, w_uv)

    return sweep_blocks(block, T, q)


def mem_kv_rows(mem, w, l):
    B, M, _ = mem.shape
    kv = rmsnorm(mem, w['mem_in_norm'][l]) @ w['w_kv_mem'][l]
    k = rmsnorm(kv[..., :MEM_W].reshape(B, M, N_MEM_HEADS, MEM_HD), w['mem_k_norm'][l]).reshape(B, M, MEM_W)
    return jnp.concatenate([k, kv[..., MEM_W:]], axis=-1)


def run_layer(x, past_nsa, past_fox_kv, past_fox_logf, past_mla, past_win, mem_kv, w, l):
    B, T, _ = x.shape
    P = past_nsa.shape[1]
    S = P + T
    pos = P + jnp.arange(T)
    h = rmsnorm(x, w['norm_mix'][l])
    proj = h @ w['w_in'][l]
    nsa_q, nsa_kv, nsa_g, fox_q, fox_kv, fox_f, mla_q, mla_c, mla_r, merge_g = jnp.split(proj, split_offsets(), axis=-1)

    g_nk = w['nsa_k_norm'][l]
    nq = rmsnorm(nsa_q.reshape(B, T, N_NSA_HEADS, HEAD_DIM), w['nsa_q_norm'][l])
    kv6 = nsa_kv.reshape(B, T, 6, HEAD_DIM)
    k_slc = rope(rmsnorm(kv6[:, :, 2:3], g_nk[1]), pos)[:, :, 0]
    k_win = rope(rmsnorm(kv6[:, :, 4:5], g_nk[2]), pos)[:, :, 0]
    nsa_rows = jnp.concatenate([kv6[:, :, 0], kv6[:, :, 1], k_slc, kv6[:, :, 3]], axis=-1)
    win_rows = jnp.concatenate([k_win, kv6[:, :, 5]], axis=-1)
    nsa_all = jnp.concatenate([past_nsa, nsa_rows], axis=1)
    win_all = jnp.concatenate([past_win, win_rows], axis=1)
    nsa_gates = jax.nn.sigmoid(nsa_g.reshape(B, T, N_NSA_HEADS, 3))
    o_nsa = nsa_attend(nq, nsa_gates, nsa_all, win_all, past_win.shape[1], g_nk[0], w['nsa_pe'][l], w['nsa_w_cmp'][l])
    new_win = win_all[:, win_all.shape[1] - min(WINDOW, S):]

    fq = rmsnorm(fox_q.reshape(B, T, N_FOX_HEADS, HEAD_DIM), w['fox_q_norm'][l])
    fox_rows = jnp.concatenate([rmsnorm(fox_kv[..., :HEAD_DIM], w['fox_k_norm'][l]), fox_kv[..., HEAD_DIM:]], axis=-1)
    logf_rows = jax.nn.log_sigmoid((fox_f + w['b_fox'][l]).astype(jnp.float32)).astype(x.dtype)
    fkv_all = jnp.concatenate([past_fox_kv, fox_rows], axis=1)
    flf_all = jnp.concatenate([past_fox_logf, logf_rows], axis=1)
    o_fox = fox_attend(fq, fkv_all[..., :HEAD_DIM], fkv_all[..., HEAD_DIM:], flf_all)

    mq = mla_q.reshape(B, T, N_MLA_HEADS, MLA_QK)
    mq = rmsnorm(jnp.concatenate([mq[..., :MLA_NOPE], rope(mq[..., MLA_NOPE:], pos)], axis=-1), w['mla_q_norm'][l])
    ckv = rmsnorm(mla_c, w['mla_kv_norm'][l])
    kr = rope(mla_r[:, :, None, :], pos)[:, :, 0]
    mla_rows = jnp.concatenate([ckv, kr], axis=-1)
    mla_all = jnp.concatenate([past_mla, mla_rows], axis=1)
    o_mla = mla_attend(mq, mla_all[..., :KV_LORA], mla_all[..., KV_LORA:], w['mla_w_uk'][l], w['mla_w_uv'][l], w['mla_k_norm'][l])

    o = jnp.stack([o_nsa.reshape(B, T, BRANCH_W), o_fox.reshape(B, T, BRANCH_W), o_mla.reshape(B, T, BRANCH_W)], axis=2)
    br = jnp.einsum('btnc,ncd->btnd', o, w['w_branch'][l])
    gate = jax.nn.sigmoid(merge_g.reshape(B, T, N_BRANCH, D_MODEL))
    x = x + jnp.sum(gate * br, axis=2) @ w['w_o'][l]

    h = rmsnorm(x, w['norm_mem'][l])
    qm = rmsnorm((h @ w['w_q_mem'][l]).reshape(B, T, N_MEM_HEADS, MEM_HD), w['mem_q_norm'][l])
    mk = mem_kv[..., :MEM_W].reshape(B, -1, N_MEM_HEADS, MEM_HD)
    mv = mem_kv[..., MEM_W:].reshape(B, -1, N_MEM_HEADS, MEM_HD)
    s = jnp.einsum('bqhd,bmhd->bhqm', qm, mk).astype(jnp.float32) * MEM_HD ** -0.5
    p = jax.nn.softmax(s, axis=-1)
    om = jnp.einsum('bhqm,bmhd->bqhd', p.astype(mv.dtype), mv).reshape(B, T, MEM_W)
    x = x + om @ w['w_o_mem'][l]

    h = rmsnorm(x, w['norm_mlp'][l])
    x = x + jnp.square(jax.nn.relu(h @ w['w_up'][l])) @ w['w_down'][l]
    return x, nsa_rows, fox_rows, logf_rows, mla_rows, new_win


def setup_inputs(seed: int = 0) -> dict:
    key = jax.random.key(seed)
    ks = iter(jax.random.split(key, 48))
    f32 = jnp.float32

    def nrm(shape, scale):
        return jax.random.normal(next(ks), shape, f32) * scale

    def gain(shape):
        return 1.0 + 0.02 * jax.random.normal(next(ks), shape, f32)

    n_pages = PAST_LEN // PAGE_SIZE
    n_used = DEC_BATCH * n_pages
    n_phys = n_used + n_used // 4
    wb = min(WINDOW, PAST_LEN)
    page_table = jax.random.permutation(next(ks), n_phys)[:n_used].reshape(DEC_BATCH, n_pages).astype(jnp.int32)
    return {
        'x_prompt': nrm((BATCH, SEQ, D_MODEL), 1.0),
        'x_sample': nrm((DEC_BATCH, DEC_SEQ, D_MODEL), 1.0),
        'cache_nsa': nrm((DEPTH, n_phys, PAGE_SIZE, 4 * HEAD_DIM), 1.0),
        'cache_fox_kv': nrm((DEPTH, n_phys, PAGE_SIZE, 2 * HEAD_DIM), 1.0),
        'cache_fox_logf': jax.nn.log_sigmoid(2.0 + jax.random.normal(next(ks), (DEPTH, n_phys, PAGE_SIZE, N_FOX_HEADS), f32)),
        'cache_mla': nrm((DEPTH, n_phys, PAGE_SIZE, KV_LORA + MLA_ROPE), 1.0),
        'state_nsa_win': nrm((DEPTH, DEC_BATCH, wb, 2 * HEAD_DIM), 1.0),
        'cache_mem': nrm((DEPTH, DEC_BATCH, N_MEM, 2 * MEM_W), 1.0),
        'page_table': page_table,
        'mem_prompt': nrm((BATCH, N_MEM, D_MODEL), 1.0),
        'norm_mix': gain((DEPTH, D_MODEL)),
        'w_in': nrm((DEPTH, D_MODEL, N_IN), D_MODEL ** -0.5),
        'b_fox': 2.0 + 0.5 * jax.random.normal(next(ks), (DEPTH, N_FOX_HEADS), f32),
        'nsa_q_norm': gain((DEPTH, HEAD_DIM)),
        'nsa_k_norm': gain((DEPTH, 3, HEAD_DIM)),
        'nsa_pe': nrm((DEPTH, 2, L_CMP, HEAD_DIM), 0.1),
        'nsa_w_cmp': nrm((DEPTH, 2, L_CMP, HEAD_DIM, HEAD_DIM), (L_CMP * HEAD_DIM) ** -0.5),
        'fox_q_norm': gain((DEPTH, HEAD_DIM)),
        'fox_k_norm': gain((DEPTH, HEAD_DIM)),
        'mla_q_norm': gain((DEPTH, MLA_QK)),
        'mla_kv_norm': gain((DEPTH, KV_LORA)),
        'mla_w_uk': nrm((DEPTH, KV_LORA, N_MLA_HEADS, MLA_NOPE), KV_LORA ** -0.5),
        'mla_w_uv': nrm((DEPTH, KV_LORA, N_MLA_HEADS, MLA_V), KV_LORA ** -0.5),
        'mla_k_norm': gain((DEPTH, MLA_QK)),
        'w_branch': nrm((DEPTH, N_BRANCH, BRANCH_W, D_MODEL), BRANCH_W ** -0.5),
        'w_o': nrm((DEPTH, D_MODEL, D_MODEL), D_MODEL ** -0.5),
        'norm_mem': gain((DEPTH, D_MODEL)),
        'mem_in_norm': gain((DEPTH, D_MODEL)),
        'w_q_mem': nrm((DEPTH, D_MODEL, MEM_W), D_MODEL ** -0.5),
        'w_kv_mem': nrm((DEPTH, D_MODEL, 2 * MEM_W), D_MODEL ** -0.5),
        'mem_q_norm': gain((DEPTH, MEM_HD)),
        'mem_k_norm': gain((DEPTH, MEM_HD)),
        'w_o_mem': nrm((DEPTH, MEM_W, D_MODEL), MEM_W ** -0.5),
        'norm_mlp': gain((DEPTH, D_MODEL)),
        'w_up': nrm((DEPTH, D_MODEL, D_FF), D_MODEL ** -0.5),
        'w_down': nrm((DEPTH, D_FF, D_MODEL), D_FF ** -0.5),
    }


def reference(x_prompt, x_sample, cache_nsa, cache_fox_kv, cache_fox_logf, cache_mla, state_nsa_win, cache_mem,
              page_table, mem_prompt, norm_mix, w_in, b_fox, nsa_q_norm, nsa_k_norm, nsa_pe, nsa_w_cmp,
              fox_q_norm, fox_k_norm, mla_q_norm, mla_kv_norm, mla_w_uk, mla_w_uv, mla_k_norm, w_branch, w_o,
              norm_mem, mem_in_norm, w_q_mem, w_kv_mem, mem_q_norm, mem_k_norm, w_o_mem, norm_mlp, w_up, w_down):
    w = dict(norm_mix=norm_mix, w_in=w_in, b_fox=b_fox, nsa_q_norm=nsa_q_norm, nsa_k_norm=nsa_k_norm,
             nsa_pe=nsa_pe, nsa_w_cmp=nsa_w_cmp, fox_q_norm=fox_q_norm, fox_k_norm=fox_k_norm,
             mla_q_norm=mla_q_norm, mla_kv_norm=mla_kv_norm, mla_w_uk=mla_w_uk, mla_w_uv=mla_w_uv,
             mla_k_norm=mla_k_norm, w_branch=w_branch, w_o=w_o, norm_mem=norm_mem, mem_in_norm=mem_in_norm,
             w_q_mem=w_q_mem, w_kv_mem=w_kv_mem, mem_q_norm=mem_q_norm, mem_k_norm=mem_k_norm,
             w_o_mem=w_o_mem, norm_mlp=norm_mlp, w_up=w_up, w_down=w_down)
    B = x_prompt.shape[0]
    dt = x_prompt.dtype

    def empty(f):
        return jnp.zeros((B, 0, f), dt)

    xp, xs = x_prompt, x_sample
    p_nsa, p_fkv, p_flf, p_mla, p_win, p_mem = [], [], [], [], [], []
    s_nsa, s_fkv, s_flf, s_mla, s_win = [], [], [], [], []
    for l in range(DEPTH):
        mem_p = mem_kv_rows(mem_prompt, w, l)
        xp, r_nsa, r_fkv, r_flf, r_mla, r_win = run_layer(
            xp, empty(4 * HEAD_DIM), empty(2 * HEAD_DIM), empty(N_FOX_HEADS), empty(KV_LORA + MLA_ROPE),
            empty(2 * HEAD_DIM), mem_p, w, l)
        p_nsa.append(r_nsa); p_fkv.append(r_fkv); p_flf.append(r_flf); p_mla.append(r_mla)
        p_win.append(r_win); p_mem.append(mem_p)
        xs, q_nsa, q_fkv, q_flf, q_mla, q_win = run_layer(
            xs, gather_pages(cache_nsa[l], page_table), gather_pages(cache_fox_kv[l], page_table),
            gather_pages(cache_fox_logf[l], page_table), gather_pages(cache_mla[l], page_table),
            state_nsa_win[l], cache_mem[l], w, l)
        s_nsa.append(q_nsa); s_fkv.append(q_fkv); s_flf.append(q_flf); s_mla.append(q_mla); s_win.append(q_win)
    return (xp, xs,
            jnp.stack(p_nsa), jnp.stack(p_fkv), jnp.stack(p_flf), jnp.stack(p_mla), jnp.stack(p_win), jnp.stack(p_mem),
            jnp.stack(s_nsa), jnp.stack(s_fkv), jnp.stack(s_flf), jnp.stack(s_mla), jnp.stack(s_win))
```

```python
import functools
import math

import jax
import jax.numpy as jnp
import numpy as np
from jax import lax
from jax.experimental import pallas as pl
from jax.experimental.pallas import tpu as pltpu

F32 = jnp.float32
MXU_DTYPE = jnp.bfloat16

HEAD_DIM = 64
N_HEADS = 8
MLA_NOPE = 64
MLA_ROPE = 32
MLA_QK = MLA_NOPE + MLA_ROPE
KV_LORA = 256
N_BRANCH = 3
L_CMP = 32
STRIDE_CMP = 16
L_SEL = 64
N_SELECT = 16
WINDOW = 512
N_MEM_HEADS = 4
MEM_HD = 128
MEM_W = N_MEM_HEADS * MEM_HD
ROPE_THETA = 10000.0
EPS = 1e-6
NEG = -1e30

P_QN, P_KVN, P_QF, P_KVF, P_QMN, P_QMR, P_CKV, P_SMALL, P_W = 0, 512, 896, 1408, 1536, 2048, 2304, 2560, 2688
S_GATE, S_LOGF, S_KR = 0, 24, 32

VMEM_LIMIT = 56 * 1024 * 1024


def _cp(*sem):
    return pltpu.CompilerParams(dimension_semantics=sem, vmem_limit_bytes=VMEM_LIMIT)


def _mm(a, b):
    return jnp.dot(a.astype(MXU_DTYPE), b.astype(MXU_DTYPE), preferred_element_type=F32)


def _mm_nt(a, b):
    return lax.dot_general(a.astype(MXU_DTYPE), b.astype(MXU_DTYPE), (((1,), (1,)), ((), ())),
                           preferred_element_type=F32)


def _split3(x):
    hi = x.astype(MXU_DTYPE)
    r = x - hi.astype(F32)
    mid = r.astype(MXU_DTYPE)
    lo = (r - mid.astype(F32)).astype(MXU_DTYPE)
    return hi, mid, lo


def _mm3(a, b_exact):
    b = b_exact.astype(MXU_DTYPE)
    return sum(jnp.dot(p, b, preferred_element_type=F32) for p in _split3(a))


def _mm3_nt_lhs_exact(a_exact, b):
    a = a_exact.astype(MXU_DTYPE)
    dn = (((1,), (1,)), ((), ()))
    return sum(lax.dot_general(a, p, dn, preferred_element_type=F32) for p in _split3(b))


def _rms(x, g):
    return x * lax.rsqrt(jnp.mean(x * x, axis=-1, keepdims=True) + EPS) * g


def _softmax_update(s, mask, m_old, l_old, acc_old, v):
    if mask is not None:
        s = jnp.where(mask, s, NEG)
    m_new = jnp.maximum(m_old, jnp.max(s, axis=-1, keepdims=True))
    alpha = jnp.exp(m_old - m_new)
    p = jnp.exp(s - m_new)
    if mask is not None:
        p = jnp.where(mask, p, 0.0)
    l_new = alpha * l_old + jnp.sum(p, axis=-1, keepdims=True)
    acc_new = alpha * acc_old + _mm(p, v)
    return m_new, l_new, acc_new


def _topk_mask(score, ids, k_top):
    sel = jnp.zeros(score.shape, jnp.bool_)
    for _ in range(k_top):
        mx = jnp.max(score, axis=-1, keepdims=True)
        idx = jnp.min(jnp.where(score == mx, ids, 1 << 30), axis=-1, keepdims=True)
        pick = ids == idx
        sel = sel | (pick & (mx > -jnp.inf))
        score = jnp.where(pick, -jnp.inf, score)
    return sel


def _rope_tab_body(pos_ref, inv_ref, o_ref):
    ang = pos_ref[...] * inv_ref[...]
    c = jnp.cos(ang)
    s = jnp.sin(ang)
    lane = lax.broadcasted_iota(jnp.int32, (ang.shape[0], 128), 1)
    first32 = (lane % 64) < 32
    first16 = (lane % 32) < 16
    c32, s32 = c[:, :128], s[:, :128]
    c16, s16 = c[:, 128:], s[:, 128:]
    o_ref[:, 0:128] = c32
    o_ref[:, 128:256] = jnp.where(first32, -s32, 0.0)
    o_ref[:, 256:384] = jnp.where(first32, 0.0, s32)
    o_ref[:, 384:512] = c16
    o_ref[:, 512:640] = jnp.where(first16, -s16, 0.0)
    o_ref[:, 640:768] = jnp.where(first16, 0.0, s16)


def _rope_tables(pos):
    T = pos.shape[0]
    inv32 = ROPE_THETA ** (-jnp.arange(32, dtype=F32) / 32)
    inv16 = ROPE_THETA ** (-jnp.arange(16, dtype=F32) / 16)
    inv = jnp.concatenate([jnp.tile(inv32, 4), jnp.tile(inv16, 8)])[None, :]
    return pl.pallas_call(
        _rope_tab_body,
        out_shape=jax.ShapeDtypeStruct((T, 768), F32),
    )(pos[:, None], inv)


def _norm_mm_body(x_ref, g_ref, w_ref, o_ref):
    o_ref[...] = _mm(_rms(x_ref[...], g_ref[...]), w_ref[...])


def _norm_matmul(x, g, w, tm):
    M, K = x.shape
    N = w.shape[1]
    return pl.pallas_call(
        _norm_mm_body,
        grid=(M // tm,),
        in_specs=[pl.BlockSpec((tm, K), lambda i: (i, 0)),
                  pl.BlockSpec((1, K), lambda i: (0, 0)),
                  pl.BlockSpec((K, N), lambda i: (0, 0))],
        out_specs=pl.BlockSpec((tm, N), lambda i: (i, 0)),
        out_shape=jax.ShapeDtypeStruct((M, N), F32),
        compiler_params=_cp("parallel"),
    )(x, g, w)


def _mem_kv_body(x_ref, g_ref, w_ref, gk_ref, o_ref):
    kv = _mm(_rms(x_ref[...], g_ref[...]), w_ref[...])
    for h in range(N_MEM_HEADS):
        sl = slice(h * MEM_HD, (h + 1) * MEM_HD)
        o_ref[:, sl] = _rms(kv[:, sl], gk_ref[...])
    o_ref[:, MEM_W:] = kv[:, MEM_W:]


def _mem_kv_rows(mem2d, g_in, w_kv, g_k, tm):
    M, K = mem2d.shape
    N = w_kv.shape[1]
    return pl.pallas_call(
        _mem_kv_body,
        grid=(M // tm,),
        in_specs=[pl.BlockSpec((tm, K), lambda i: (i, 0)),
                  pl.BlockSpec((1, K), lambda i: (0, 0)),
                  pl.BlockSpec((K, N), lambda i: (0, 0)),
                  pl.BlockSpec((1, MEM_HD), lambda i: (0, 0))],
        out_specs=pl.BlockSpec((tm, N), lambda i: (i, 0)),
        out_shape=jax.ShapeDtypeStruct((M, N), F32),
        compiler_params=_cp("parallel"),
    )(mem2d, g_in, w_kv, g_k)


def _post_body(p_ref, tab_ref, gqn_ref, gkv_ref, mkv_ref, gqf_ref, gfk_ref, gmn_ref, gmr_ref, gckv_ref,
               bfox_ref, g64_ref, gx_ref, gxt_ref, g32_ref,
               nq_ref, nqr_ref, fq_ref, mq_ref, nsa_ref, win_ref, fox_ref, mla_ref, small_ref):
    tab = tab_ref[...]
    c32, sp32, sm32, c16, sp16, sm16 = [tab[:, i * 128:(i + 1) * 128] for i in range(6)]

    def tile(t, n):
        return t if n == 1 else jnp.concatenate([t] * n, axis=1)

    def rope(x, c, sp, sm, s):
        w = x.shape[1]
        n = w // 128
        return x * tile(c, n) + pltpu.roll(x, w - s, 1) * tile(sp, n) + pltpu.roll(x, s, 1) * tile(sm, n)

    g64 = g64_ref[...]
    scale = HEAD_DIM ** -0.5

    x = p_ref[:, P_QN:P_QN + 512]
    y = x * lax.rsqrt(_mm3(x * x, g64) / HEAD_DIM + EPS) * gqn_ref[...]
    yr = rope(y, c32, sp32, sm32, 32)
    xf = p_ref[:, P_QF:P_QF + 512]
    yf = xf * lax.rsqrt(_mm3(xf * xf, g64) / HEAD_DIM + EPS) * gqf_ref[...]
    for h in range(N_HEADS):
        sl = slice(h * HEAD_DIM, (h + 1) * HEAD_DIM)
        nq_ref[h] = (y[:, sl] * scale).astype(nq_ref.dtype)
        nqr_ref[h] = (yr[:, sl] * scale).astype(nqr_ref.dtype)
        fq_ref[h] = (yf[:, sl] * scale).astype(fq_ref.dtype)

    x = p_ref[:, P_KVN:P_KVN + 384]
    normed = mkv_ref[...] > 0.5
    yn = x * lax.rsqrt(_mm3(x * x, g64[:384, :384]) / HEAD_DIM + EPS) * gkv_ref[...]
    y = jnp.where(normed, rope(yn, c32, sp32, sm32, 32), x)
    nsa_ref[...] = y[:, :256]
    win_ref[...] = y[:, 256:]

    x = p_ref[:, P_KVF:P_KVF + 128]
    lane = lax.broadcasted_iota(jnp.int32, x.shape, 1)
    yn = x * lax.rsqrt(_mm3(x * x, g64[:128, :128]) / HEAD_DIM + EPS) * gfk_ref[...]
    fox_ref[...] = jnp.where(lane < HEAD_DIM, yn, x)

    xn = p_ref[:, P_QMN:P_QMN + 512]
    xr = rope(p_ref[:, P_QMR:P_QMR + 256], c16, sp16, sm16, 16)
    xn2, xr2 = xn * xn, xr * xr
    ms_n = (_mm3(xn2, g64) + _mm3(xr2, gx_ref[...])) / MLA_QK
    ms_r = (_mm3(xn2, gxt_ref[...]) + _mm3(xr2, g32_ref[...])) / MLA_QK
    yn = xn * lax.rsqrt(ms_n + EPS) * gmn_ref[...]
    yr = xr * lax.rsqrt(ms_r + EPS) * gmr_ref[...]
    for h in range(N_HEADS):
        mq_ref[h, :, 0:MLA_NOPE] = yn[:, h * MLA_NOPE:(h + 1) * MLA_NOPE].astype(mq_ref.dtype)
        mq_ref[h, :, MLA_NOPE:MLA_QK] = yr[:, h * MLA_ROPE:(h + 1) * MLA_ROPE].astype(mq_ref.dtype)

    x = p_ref[:, P_SMALL:P_SMALL + 128]
    sig = jax.nn.sigmoid(x)
    lf = jax.nn.log_sigmoid(x + bfox_ref[...])
    rr = rope(x, c16, sp16, sm16, 16)
    small = jnp.where(lane < S_LOGF, sig, jnp.where(lane < S_KR, lf, jnp.where(lane < S_KR + MLA_ROPE, rr, 0.0)))
    small_ref[...] = small

    x = p_ref[:, P_CKV:P_CKV + KV_LORA]
    mla_ref[:, 0:KV_LORA] = _rms(x, gckv_ref[...])
    mla_ref[:, KV_LORA:KV_LORA + MLA_ROPE] = small[:, S_KR:S_KR + MLA_ROPE]


def _post(p, tab, T, consts, tm):
    M = p.shape[0]
    if T >= tm:
        nt = T // tm
        tab_map = lambda i: (i % nt, 0)
    else:
        tab_map = lambda i: (0, 0)
    row = lambda w: pl.BlockSpec((tm, w), lambda i: (i, 0))
    head = lambda d: pl.BlockSpec((N_HEADS, tm, d), lambda i: (0, i, 0))
    full = lambda a: pl.BlockSpec(a.shape, lambda i: (0,) * a.ndim)
    qd = MXU_DTYPE
    out_shape = [jax.ShapeDtypeStruct((N_HEADS, M, HEAD_DIM), qd)] * 3 + [
        jax.ShapeDtypeStruct((N_HEADS, M, MLA_QK), qd),
        jax.ShapeDtypeStruct((M, 256), F32), jax.ShapeDtypeStruct((M, 128), F32),
        jax.ShapeDtypeStruct((M, 128), F32), jax.ShapeDtypeStruct((M, KV_LORA + MLA_ROPE), F32),
        jax.ShapeDtypeStruct((M, 128), F32)]
    out_specs = [head(HEAD_DIM)] * 3 + [head(MLA_QK), row(256), row(128), row(128), row(KV_LORA + MLA_ROPE), row(128)]
    return pl.pallas_call(
        _post_body,
        grid=(M // tm,),
        in_specs=[row(P_W), pl.BlockSpec((tm, 768), tab_map)] + [full(c) for c in consts],
        out_specs=out_specs,
        out_shape=out_shape,
        compiler_params=_cp("parallel"),
    )(p, tab, *consts)


def _compress(load_xl, nch, pe_ref, w_ref, g_ref):
    acc_a = jnp.zeros((nch, 128), F32)
    acc_b = jnp.zeros((nch, 128), F32)
    for l in range(STRIDE_CMP):
        xl = load_xl(l)
        acc_a = acc_a + _mm(xl + pe_ref[l:l + 1, :], w_ref[l])
        acc_b = acc_b + _mm(xl + pe_ref[STRIDE_CMP + l:STRIDE_CMP + l + 1, :], w_ref[STRIDE_CMP + l])
    kcv = acc_a + pltpu.roll(acc_b, nch - 1, 0)
    lane = lax.broadcasted_iota(jnp.int32, kcv.shape, 1)
    is_k = lane < HEAD_DIM
    ms = jnp.sum(jnp.where(is_k, kcv * kcv, 0.0), axis=-1, keepdims=True) / HEAD_DIM
    return jnp.where(is_k, kcv * lax.rsqrt(ms + EPS) * g_ref[...], kcv)


def _cmp_body(rows_ref, pe_ref, w_ref, g_ref, o_ref, *, nch):
    o_ref[...] = _compress(lambda l: rows_ref[pl.ds(l, nch, stride=STRIDE_CMP), :], nch, pe_ref, w_ref, g_ref)


def _nsa_compress(nsa_rows, B, S, pe, wblk, gk):
    nch = S // STRIDE_CMP
    return pl.pallas_call(
        functools.partial(_cmp_body, nch=nch),
        grid=(B,),
        in_specs=[pl.BlockSpec((S, 128), lambda b: (b, 0)),
                  pl.BlockSpec(pe.shape, lambda b: (0, 0)),
                  pl.BlockSpec(wblk.shape, lambda b: (0, 0, 0)),
                  pl.BlockSpec((1, 128), lambda b: (0, 0))],
        out_specs=pl.BlockSpec((None, nch, 128), lambda b: (b, 0, 0)),
        out_shape=jax.ShapeDtypeStruct((B, nch, 128), F32),
        compiler_params=_cp("parallel"),
    )(nsa_rows, pe, wblk, gk)


def _nsa_prompt_body(nq_ref, nqr_ref, small_ref, rows_ref, win_ref, kcv_ref, selw_ref, et_ref, o_ref,
                     m_ref, l_ref, acc_ref, out_ref, *, tq, tk, tw, k_top):
    qi = pl.program_id(1)
    q0 = qi * tq
    qpos = q0 + lax.broadcasted_iota(jnp.int32, (tq, 1), 0)
    nch = kcv_ref.shape[0]
    nsp = selw_ref.shape[1]

    kc = kcv_ref[:, 0:HEAD_DIM]
    vc = kcv_ref[:, HEAD_DIM:]
    cmp_end = lax.broadcasted_iota(jnp.int32, (1, nch), 1) * STRIDE_CMP + (L_CMP - 1)
    mask_c = cmp_end <= qpos
    psum = jnp.zeros((tq, nch), F32)
    for h in range(N_HEADS):
        s = jnp.where(mask_c, _mm_nt(nq_ref[h], kc), NEG)
        p = jnp.where(mask_c, jnp.exp(s - jnp.max(s, axis=-1, keepdims=True)), 0.0)
        p = p / jnp.maximum(jnp.sum(p, axis=-1, keepdims=True), 1e-30)
        out_ref[h] = small_ref[:, S_GATE + 3 * h:S_GATE + 3 * h + 1] * _mm(p, vc)
        psum = psum + p
    hi, mid, lo = _split3(psum)
    selw = selw_ref[...]
    imp = (jnp.dot(hi, selw, preferred_element_type=F32) + jnp.dot(mid, selw, preferred_element_type=F32)
           + jnp.dot(lo, selw, preferred_element_type=F32))

    ids = lax.broadcasted_iota(jnp.int32, (1, nsp), 1)
    cur = qpos // L_SEL
    forced = (ids == 0) | (ids == cur) | (ids == cur - 1)
    avail = ids * L_SEL <= qpos
    score = jnp.where(forced, jnp.inf, jnp.where(avail, imp, -jnp.inf))
    selm = _topk_mask(score, ids, k_top).astype(MXU_DTYPE)

    def reset():
        m_ref[...] = jnp.full(m_ref.shape, NEG, F32)
        l_ref[...] = jnp.zeros(l_ref.shape, F32)
        acc_ref[...] = jnp.zeros(acc_ref.shape, F32)

    def attend(q_ref, k, v, mask):
        for h in range(N_HEADS):
            m, l, a = _softmax_update(_mm_nt(q_ref[h], k), mask, m_ref[h], l_ref[h], acc_ref[h], v)
            m_ref[h] = m
            l_ref[h] = l
            acc_ref[h] = a

    def flush(gate_idx):
        for h in range(N_HEADS):
            g = small_ref[:, S_GATE + 3 * h + gate_idx:S_GATE + 3 * h + gate_idx + 1]
            out_ref[h] = out_ref[h] + g * (acc_ref[h] / jnp.maximum(l_ref[h], 1e-30))

    reset()

    def sel_step(kt, carry):
        k0 = pl.multiple_of(kt * tk, tk)
        blk = _mm_nt(selm, et_ref[pl.ds(k0, tk), :]) > 0.5
        kpos = k0 + lax.broadcasted_iota(jnp.int32, (1, tk), 1)
        attend(nqr_ref, rows_ref[pl.ds(k0, tk), 128:192], rows_ref[pl.ds(k0, tk), 192:256], blk & (kpos <= qpos))
        return carry

    lax.fori_loop(0, (q0 + tq - 1) // tk + 1, sel_step, 0)
    flush(1)

    reset()

    def win_step(kt, carry):
        k0 = pl.multiple_of(kt * tw, tw)
        kpos = k0 + lax.broadcasted_iota(jnp.int32, (1, tw), 1)
        attend(nqr_ref, win_ref[pl.ds(k0, tw), 0:HEAD_DIM], win_ref[pl.ds(k0, tw), HEAD_DIM:],
               (kpos > qpos - WINDOW) & (kpos <= qpos))
        return carry

    lax.fori_loop(jnp.maximum(q0 - WINDOW, 0) // tw, (q0 + tq - 1) // tw + 1, win_step, 0)
    flush(2)

    for h in range(N_HEADS):
        o_ref[:, h * HEAD_DIM:(h + 1) * HEAD_DIM] = out_ref[h]


def _nsa_prompt(nq, nqr, small, nsa_rows, win_rows, kcv, selw, et, B, T, tq):
    nqt = T // tq
    tk = min(256, T)
    tw = min(128, T)
    n_sel = -(-T // L_SEL)
    k_top = min(N_SELECT, n_sel)
    nch = kcv.shape[1]
    head = pl.BlockSpec((N_HEADS, tq, HEAD_DIM), lambda b, i: (0, b * nqt + i, 0))
    return pl.pallas_call(
        functools.partial(_nsa_prompt_body, tq=tq, tk=tk, tw=tw, k_top=k_top),
        grid=(B, nqt),
        in_specs=[head, head,
                  pl.BlockSpec((tq, 128), lambda b, i: (b * nqt + i, 0)),
                  pl.BlockSpec((T, 256), lambda b, i: (b, 0)),
                  pl.BlockSpec((T, 128), lambda b, i: (b, 0)),
                  pl.BlockSpec((None, nch, 128), lambda b, i: (b, 0, 0)),
                  pl.BlockSpec(selw.shape, lambda b, i: (0, 0)),
                  pl.BlockSpec(et.shape, lambda b, i: (0, 0))],
        out_specs=pl.BlockSpec((tq, N_HEADS * HEAD_DIM), lambda b, i: (b * nqt + i, 0)),
        out_shape=jax.ShapeDtypeStruct((B * T, N_HEADS * HEAD_DIM), F32),
        scratch_shapes=[pltpu.VMEM((N_HEADS, tq, 1), F32), pltpu.VMEM((N_HEADS, tq, 1), F32),
                        pltpu.VMEM((N_HEADS, tq, HEAD_DIM), F32), pltpu.VMEM((N_HEADS, tq, HEAD_DIM), F32)],
        compiler_params=_cp("parallel", "arbitrary"),
    )(nq, nqr, small, nsa_rows, win_rows, kcv, selw, et)


def _cumsum_body(small_ref, cst_ref, *, S, tc):
    r = lax.broadcasted_iota(jnp.int32, (tc, tc), 0)
    c = lax.broadcasted_iota(jnp.int32, (tc, tc), 1)
    upper = (r <= c).astype(MXU_DTYPE)

    def step(i, carry):
        k0 = pl.multiple_of(i * tc, tc)
        xt = small_ref[pl.ds(k0, tc), :].T
        ct = _mm3(xt, upper) + carry
        cst_ref[i] = ct[S_LOGF:S_LOGF + N_HEADS, :]
        return ct[:, tc - 1:tc]

    lax.fori_loop(0, S // tc, step, jnp.zeros((128, 1), F32))


def _fox_cumsum(small, B, S):
    tc = min(256, S)
    return pl.pallas_call(
        functools.partial(_cumsum_body, S=S, tc=tc),
        grid=(B,),
        in_specs=[pl.BlockSpec((S, 128), lambda b: (b, 0))],
        out_specs=pl.BlockSpec((None, S // tc, N_HEADS, tc), lambda b: (b, 0, 0, 0)),
        out_shape=jax.ShapeDtypeStruct((B, S // tc, N_HEADS, tc), F32),
        compiler_params=_cp("parallel"),
    )(small)


def _flash_body(*refs, tq, tk, dv, per_head, decay, scale):
    if decay:
        q_ref, k_ref, c_ref, o_ref, m_ref, l_ref, acc_ref = refs
        v_ref = None
    elif per_head:
        q_ref, k_ref, v_ref, o_ref, m_ref, l_ref, acc_ref = refs
    else:
        q_ref, k_ref, o_ref, m_ref, l_ref, acc_ref = refs
        v_ref = None
    qi = pl.program_id(1)
    kj = pl.program_id(2)
    q0 = qi * tq
    k0 = kj * tk

    @pl.when(kj == 0)
    def _():
        m_ref[...] = jnp.full(m_ref.shape, NEG, F32)
        l_ref[...] = jnp.zeros(l_ref.shape, F32)
        acc_ref[...] = jnp.zeros(acc_ref.shape, F32)

    @pl.when(k0 <= q0 + tq - 1)
    def _():
        qpos = q0 + lax.broadcasted_iota(jnp.int32, (tq, 1), 0)
        kpos = k0 + lax.broadcasted_iota(jnp.int32, (1, tk), 1)
        mask = kpos <= qpos
        for h in range(N_HEADS):
            if per_head:
                k, v = k_ref[h], v_ref[h]
            else:
                k, v = k_ref[:, 0:HEAD_DIM], k_ref[:, HEAD_DIM:2 * HEAD_DIM]
            s = _mm_nt(q_ref[h], k)
            if scale != 1.0:
                s = s * scale
            if decay:
                s = s - c_ref[h:h + 1, :]
            m, l, a = _softmax_update(s, mask, m_ref[h], l_ref[h], acc_ref[h], v)
            m_ref[h] = m
            l_ref[h] = l
            acc_ref[h] = a

    @pl.when(kj == pl.num_programs(2) - 1)
    def _():
        for h in range(N_HEADS):
            o_ref[:, h * dv:(h + 1) * dv] = acc_ref[h] / jnp.maximum(l_ref[h], 1e-30)


def _flash_prompt(q, k, v, cst, B, T, tq, tk, *, scale):
    nqt, nkt = T // tq, T // tk
    d = q.shape[-1]
    per_head = v is not None
    decay = cst is not None
    last = lambda i: (i * tq + tq - 1) // tk
    in_specs = [pl.BlockSpec((N_HEADS, tq, d), lambda b, i, j: (0, b * nqt + i, 0))]
    args = [q]
    if per_head:
        dv = v.shape[-1]
        in_specs += [pl.BlockSpec((None, N_HEADS, tk, d), lambda b, i, j: (b, 0, jnp.minimum(j, last(i)), 0)),
                     pl.BlockSpec((None, N_HEADS, tk, dv), lambda b, i, j: (b, 0, jnp.minimum(j, last(i)), 0))]
        args += [k, v]
    else:
        dv = HEAD_DIM
        in_specs += [pl.BlockSpec((tk, 128), lambda b, i, j: (b * nkt + jnp.minimum(j, last(i)), 0))]
        args += [k]
    if decay:
        assert cst.shape[-1] == tk
        in_specs += [pl.BlockSpec((None, None, N_HEADS, tk), lambda b, i, j: (b, jnp.minimum(j, last(i)), 0, 0))]
        args += [cst]
    return pl.pallas_call(
        functools.partial(_flash_body, tq=tq, tk=tk, dv=dv, per_head=per_head, decay=decay, scale=scale),
        grid=(B, nqt, nkt),
        in_specs=in_specs,
        out_specs=pl.BlockSpec((tq, N_HEADS * dv), lambda b, i, j: (b * nqt + i, 0)),
        out_shape=jax.ShapeDtypeStruct((B * T, N_HEADS * dv), F32),
        scratch_shapes=[pltpu.VMEM((N_HEADS, tq, 1), F32), pltpu.VMEM((N_HEADS, tq, 1), F32),
                        pltpu.VMEM((N_HEADS, tq, dv), F32)],
        compiler_params=_cp("parallel", "parallel", "arbitrary"),
    )(*args)


def _mla_kv_body(r_ref, wuk_ref, wuv_ref, gkn_ref, gkr_ref, g64_ref, k_ref, v_ref):
    ckv = r_ref[:, 0:KV_LORA]
    kr = r_ref[:, KV_LORA:KV_LORA + MLA_ROPE]
    kn = _mm(ckv, wuk_ref[...])
    ss = _mm3(kn * kn, g64_ref[...]) + jnp.sum(kr * kr, axis=-1, keepdims=True)
    inv = lax.rsqrt(ss / MLA_QK + EPS)
    knn = kn * gkn_ref[...] * inv
    krg = kr * gkr_ref[...]
    vv = _mm(ckv, wuv_ref[...])
    for h in range(N_HEADS):
        sl = slice(h * HEAD_DIM, (h + 1) * HEAD_DIM)
        k_ref[h, :, 0:MLA_NOPE] = knn[:, sl].astype(k_ref.dtype)
        k_ref[h, :, MLA_NOPE:MLA_QK] = (krg * inv[:, h * HEAD_DIM:h * HEAD_DIM + MLA_ROPE]).astype(k_ref.dtype)
        v_ref[h] = vv[:, sl].astype(v_ref.dtype)


def _mla_kv(mla_rows, B, T, wuk, wuv, gkn, gkr, g64, tk):
    nkt = T // tk
    full = lambda a: pl.BlockSpec(a.shape, lambda b, j: (0,) * a.ndim)
    return pl.pallas_call(
        _mla_kv_body,
        grid=(B, nkt),
        in_specs=[pl.BlockSpec((tk, KV_LORA + MLA_ROPE), lambda b, j: (b * nkt + j, 0)),
                  full(wuk), full(wuv), full(gkn), full(gkr), full(g64)],
        out_specs=[pl.BlockSpec((None, N_HEADS, tk, MLA_QK), lambda b, j: (b, 0, j, 0)),
                   pl.BlockSpec((None, N_HEADS, tk, HEAD_DIM), lambda b, j: (b, 0, j, 0))],
        out_shape=[jax.ShapeDtypeStruct((B, N_HEADS, T, MLA_QK), MXU_DTYPE),
                   jax.ShapeDtypeStruct((B, N_HEADS, T, HEAD_DIM), MXU_DTYPE)],
        compiler_params=_cp("parallel", "parallel"),
    )(mla_rows, wuk, wuv, gkn, gkr, g64)


def _merge_body(x_ref, o1_ref, o2_ref, o3_ref, g_ref, wb_ref, wo_ref, out_ref):
    d = x_ref.shape[1]
    acc = jnp.zeros(x_ref.shape, F32)
    for b, o_ref in enumerate((o1_ref, o2_ref, o3_ref)):
        acc = acc + jax.nn.sigmoid(g_ref[:, b * d:(b + 1) * d]) * _mm(o_ref[...], wb_ref[b])
    out_ref[...] = x_ref[...] + _mm(acc, wo_ref[...])


def _merge(x, o1, o2, o3, g, wb, wo, tm):
    M, D = x.shape
    row = lambda w: pl.BlockSpec((tm, w), lambda i: (i, 0))
    return pl.pallas_call(
        _merge_body,
        grid=(M // tm,),
        in_specs=[row(D), row(o1.shape[1]), row(o2.shape[1]), row(o3.shape[1]), row(g.shape[1]),
                  pl.BlockSpec(wb.shape, lambda i: (0, 0, 0)), pl.BlockSpec(wo.shape, lambda i: (0, 0))],
        out_specs=row(D),
        out_shape=jax.ShapeDtypeStruct((M, D), F32),
        compiler_params=_cp("parallel"),
    )(x, o1, o2, o3, g, wb, wo)


def _mem_attn_body(x_ref, kv_ref, g_ref, wq_ref, gq_ref, wo_ref, o_ref):
    x = x_ref[...]
    q = _mm(_rms(x, g_ref[...]), wq_ref[...])
    outs = []
    for h in range(N_MEM_HEADS):
        sl = slice(h * MEM_HD, (h + 1) * MEM_HD)
        qh = _rms(q[:, sl], gq_ref[...])
        s = _mm_nt(qh, kv_ref[:, sl]) * MEM_HD ** -0.5
        p = jnp.exp(s - jnp.max(s, axis=-1, keepdims=True))
        p = p / jnp.sum(p, axis=-1, keepdims=True)
        outs.append(_mm(p, kv_ref[:, MEM_W + h * MEM_HD:MEM_W + (h + 1) * MEM_HD]))
    o_ref[...] = x + _mm(jnp.concatenate(outs, axis=1), wo_ref[...])


def _mem_attn(x, mem_kv, g, wq, gq, wo, B, T, tm):
    D = x.shape[1]
    nt = T // tm
    n_mem = mem_kv.shape[1]
    full = lambda a: pl.BlockSpec(a.shape, lambda b, i: (0,) * a.ndim)
    return pl.pallas_call(
        _mem_attn_body,
        grid=(B, nt),
        in_specs=[pl.BlockSpec((tm, D), lambda b, i: (b * nt + i, 0)),
                  pl.BlockSpec((None, n_mem, 2 * MEM_W), lambda b, i: (b, 0, 0)),
                  full(g), full(wq), full(gq), full(wo)],
        out_specs=pl.BlockSpec((tm, D), lambda b, i: (b * nt + i, 0)),
        out_shape=jax.ShapeDtypeStruct(x.shape, F32),
        compiler_params=_cp("parallel", "parallel"),
    )(x, mem_kv, g, wq, gq, wo)


def _mlp_body(x_ref, g_ref, wu_ref, wd_ref, o_ref, h_ref, acc_ref):
    j = pl.program_id(1)

    @pl.when(j == 0)
    def _():
        h_ref[...] = _rms(x_ref[...], g_ref[...]).astype(h_ref.dtype)
        acc_ref[...] = jnp.zeros(acc_ref.shape, F32)

    u = jnp.dot(h_ref[...], wu_ref[...], preferred_element_type=F32)
    acc_ref[...] += _mm(jnp.square(jnp.maximum(u, 0.0)), wd_ref[...])

    @pl.when(j == pl.num_programs(1) - 1)
    def _():
        o_ref[...] = x_ref[...] + acc_ref[...]


def _mlp(x, g, wu, wd, tm, tf):
    M, D = x.shape
    FF = wu.shape[1]
    return pl.pallas_call(
        _mlp_body,
        grid=(M // tm, FF // tf),
        in_specs=[pl.BlockSpec((tm, D), lambda i, j: (i, 0)),
                  pl.BlockSpec((1, D), lambda i, j: (0, 0)),
                  pl.BlockSpec((D, tf), lambda i, j: (0, j)),
                  pl.BlockSpec((tf, D), lambda i, j: (j, 0))],
        out_specs=pl.BlockSpec((tm, D), lambda i, j: (i, 0)),
        out_shape=jax.ShapeDtypeStruct((M, D), F32),
        scratch_shapes=[pltpu.VMEM((tm, D), MXU_DTYPE), pltpu.VMEM((tm, D), F32)],
        compiler_params=_cp("parallel", "arbitrary"),
    )(x, g, wu, wd)


def _page_copies(pt_ref, bb, slot, n_pages, page, pairs, start):
    def body(p, carry):
        pg = pt_ref[bb, p]
        for src_fn, dst_fn, sem in pairs:
            cp = pltpu.make_async_copy(src_fn(pg), dst_fn(slot, p), sem.at[slot])
            if start:
                cp.start()
            else:
                cp.wait()
        return carry

    lax.fori_loop(0, n_pages, body, 0)


def _gather_pipeline(pt_ref, n_pages, page, pairs):
    b = pl.program_id(0)
    nb = pl.num_programs(0)

    @pl.when(b == 0)
    def _():
        _page_copies(pt_ref, 0, 0, n_pages, page, pairs, True)

    @pl.when(b + 1 < nb)
    def _():
        _page_copies(pt_ref, b + 1, (b + 1) % 2, n_pages, page, pairs, True)

    slot = b % 2
    _page_copies(pt_ref, b, slot, n_pages, page, pairs, False)
    return slot


def _row_mod(rows, T):
    return lax.broadcasted_iota(jnp.int32, (rows, 1), 0) % T


def _heads_to_lanes(o_ref, o, T, dv):
    for h in range(N_HEADS):
        o_ref[:, h * dv:(h + 1) * dv] = o[h * T:(h + 1) * T, :]


def _fox_sample_body(pt_ref, q_ref, newkv_ref, newsmall_ref, kv_hbm, lf_hbm, o_ref, kvbuf, lfbuf, ct_ref, sem_kv,
                     sem_lf, *, layer, n_pages, page, tk, T):
    pairs = [(lambda pg: kv_hbm.at[layer, pg], lambda s, p: kvbuf.at[s, pl.ds(p * page, page)], sem_kv),
             (lambda pg: lf_hbm.at[layer, pg], lambda s, p: lfbuf.at[s, p], sem_lf)]
    slot = _gather_pipeline(pt_ref, n_pages, page, pairs)
    R = N_HEADS * T
    P = n_pages * page

    eye = (lax.broadcasted_iota(jnp.int32, (N_HEADS, N_HEADS), 0)
           == lax.broadcasted_iota(jnp.int32, (N_HEADS, N_HEADS), 1)).astype(MXU_DTYPE)
    upper = (lax.broadcasted_iota(jnp.int32, (page, page), 0)
             <= lax.broadcasted_iota(jnp.int32, (page, page), 1)).astype(MXU_DTYPE)

    def cs_step(p, carry):
        ct = _mm3(_mm3_nt_lhs_exact(eye, lfbuf[slot, p]), upper) + carry
        ct_ref[p] = ct
        return ct[:, page - 1:page]

    total = lax.fori_loop(0, n_pages, cs_step, jnp.zeros((N_HEADS, 1), F32))
    upper_t = (lax.broadcasted_iota(jnp.int32, (T, T), 0) <= lax.broadcasted_iota(jnp.int32, (T, T), 1)).astype(MXU_DTYPE)
    ct_new = _mm3(_mm3_nt_lhs_exact(eye, newsmall_ref[:, S_LOGF:S_LOGF + N_HEADS]), upper_t) + total

    q = q_ref[...].reshape(R, HEAD_DIM)
    ppt = tk // page

    def step(j, carry):
        m, l, acc = carry
        k0 = pl.multiple_of(j * tk, tk)
        kv = kvbuf[slot, pl.ds(k0, tk), :]
        ct = jnp.concatenate([ct_ref[j * ppt + i] for i in range(ppt)], axis=1)
        s = (_mm_nt(q, kv[:, 0:HEAD_DIM]).reshape(N_HEADS, T, tk) - ct[:, None, :]).reshape(R, tk)
        return _softmax_update(s, None, m, l, acc, kv[:, HEAD_DIM:])

    init = (jnp.full((R, 1), NEG, F32), jnp.zeros((R, 1), F32), jnp.zeros((R, HEAD_DIM), F32))
    m, l, acc = lax.fori_loop(0, P // tk, step, init)

    knew = newkv_ref[...]
    s = (_mm_nt(q, knew[:, 0:HEAD_DIM]).reshape(N_HEADS, T, T) - ct_new[:, None, :]).reshape(R, T)
    mask = lax.broadcasted_iota(jnp.int32, (1, T), 1) <= _row_mod(R, T)
    m, l, acc = _softmax_update(s, mask, m, l, acc, knew[:, HEAD_DIM:])
    _heads_to_lanes(o_ref, acc / jnp.maximum(l, 1e-30), T, HEAD_DIM)


def _fox_sample(page_table, fq, fox_rows, small, cache_kv, cache_lf, layer, B, T):
    n_pages = page_table.shape[1]
    page = cache_kv.shape[2]
    P = n_pages * page
    tk = min(512, P)
    grid_spec = pltpu.PrefetchScalarGridSpec(
        num_scalar_prefetch=1,
        grid=(B,),
        in_specs=[pl.BlockSpec((N_HEADS, T, HEAD_DIM), lambda b, pt: (0, b, 0)),
                  pl.BlockSpec((T, 128), lambda b, pt: (b, 0)),
                  pl.BlockSpec((T, 128), lambda b, pt: (b, 0)),
                  pl.BlockSpec(memory_space=pl.ANY),
                  pl.BlockSpec(memory_space=pl.ANY)],
        out_specs=pl.BlockSpec((T, N_HEADS * HEAD_DIM), lambda b, pt: (b, 0)),
        scratch_shapes=[pltpu.VMEM((2, P, 128), F32),
                        pltpu.VMEM((2, n_pages, page, N_HEADS), F32),
                        pltpu.VMEM((n_pages, N_HEADS, page), F32),
                        pltpu.SemaphoreType.DMA((2,)), pltpu.SemaphoreType.DMA((2,))])
    return pl.pallas_call(
        functools.partial(_fox_sample_body, layer=layer, n_pages=n_pages, page=page, tk=tk, T=T),
        grid_spec=grid_spec,
        out_shape=jax.ShapeDtypeStruct((B * T, N_HEADS * HEAD_DIM), F32),
        compiler_params=_cp("arbitrary"),
    )(page_table, fq, fox_rows, small, cache_kv, cache_lf)


def _mla_sample_body(pt_ref, q_ref, new_ref, wukt_ref, wuv_ref, gkn_ref, gkr_ref, c_hbm, o_ref, buf, sem,
                     *, layer, n_pages, page, tk, T):
    pairs = [(lambda pg: c_hbm.at[layer, pg], lambda s, p: buf.at[s, pl.ds(p * page, page)], sem)]
    slot = _gather_pipeline(pt_ref, n_pages, page, pairs)
    R = N_HEADS * T
    P = n_pages * page
    scale = MLA_QK ** -0.5

    ql, qr = [], []
    for h in range(N_HEADS):
        qh = q_ref[h].astype(F32)
        ql.append(_mm(qh[:, 0:MLA_NOPE] * gkn_ref[...], wukt_ref[h]))
        qr.append(qh[:, MLA_NOPE:MLA_QK] * gkr_ref[...])
    ql = jnp.concatenate(ql, axis=0)
    qr = jnp.concatenate(qr, axis=0)
    wukt = wukt_ref[...].reshape(N_HEADS * MLA_NOPE, KV_LORA)
    ones = jnp.ones((N_HEADS, MLA_ROPE), MXU_DTYPE)

    def scores(rows, n):
        ckv = rows[:, 0:KV_LORA]
        kr = rows[:, KV_LORA:KV_LORA + MLA_ROPE]
        s = _mm_nt(ql, ckv) + _mm_nt(qr, kr)
        knt = _mm_nt(wukt, ckv)
        ssq = jnp.sum((knt * knt).reshape(N_HEADS, MLA_NOPE, n), axis=1) + _mm3_nt_lhs_exact(ones, kr * kr)
        inv = lax.rsqrt(ssq / MLA_QK + EPS)
        return (s.reshape(N_HEADS, T, n) * (inv * scale)[:, None, :]).reshape(R, n), ckv

    def step(j, carry):
        m, l, acc = carry
        k0 = pl.multiple_of(j * tk, tk)
        s, ckv = scores(buf[slot, pl.ds(k0, tk), :], tk)
        return _softmax_update(s, None, m, l, acc, ckv)

    init = (jnp.full((R, 1), NEG, F32), jnp.zeros((R, 1), F32), jnp.zeros((R, KV_LORA), F32))
    m, l, acc = lax.fori_loop(0, P // tk, step, init)
    s, ckv = scores(new_ref[...], T)
    mask = lax.broadcasted_iota(jnp.int32, (1, T), 1) <= _row_mod(R, T)
    m, l, acc = _softmax_update(s, mask, m, l, acc, ckv)
    lat = acc / jnp.maximum(l, 1e-30)
    for h in range(N_HEADS):
        o_ref[:, h * HEAD_DIM:(h + 1) * HEAD_DIM] = _mm(lat[h * T:(h + 1) * T, :], wuv_ref[h])


def _mla_sample(page_table, mq, mla_rows, wukt, wuv, gkn, gkr, cache, layer, B, T):
    n_pages = page_table.shape[1]
    page = cache.shape[2]
    P = n_pages * page
    F = cache.shape[3]
    tk = min(256, P)
    full = lambda a: pl.BlockSpec(a.shape, lambda b, pt: (0,) * a.ndim)
    grid_spec = pltpu.PrefetchScalarGridSpec(
        num_scalar_prefetch=1,
        grid=(B,),
        in_specs=[pl.BlockSpec((N_HEADS, T, MLA_QK), lambda b, pt: (0, b, 0)),
                  pl.BlockSpec((T, F), lambda b, pt: (b, 0)),
                  full(wukt), full(wuv), full(gkn), full(gkr),
                  pl.BlockSpec(memory_space=pl.ANY)],
        out_specs=pl.BlockSpec((T, N_HEADS * HEAD_DIM), lambda b, pt: (b, 0)),
        scratch_shapes=[pltpu.VMEM((2, P, F), F32), pltpu.SemaphoreType.DMA((2,))])
    return pl.pallas_call(
        functools.partial(_mla_sample_body, layer=layer, n_pages=n_pages, page=page, tk=tk, T=T),
        grid_spec=grid_spec,
        out_shape=jax.ShapeDtypeStruct((B * T, N_HEADS * HEAD_DIM), F32),
        compiler_params=_cp("arbitrary"),
    )(page_table, mq, mla_rows, wukt, wuv, gkn, gkr, cache)


def _nsa_sample_body(pt_ref, nq_ref, nqr_ref, small_ref, newrows_ref, newwin_ref, pastwin_ref, pe_ref, w_ref, g_ref,
                     selw_ref, et_ref, c_hbm, o_ref, winout_ref, bufc, bufs, sem_c, sem_s, *, layer, n_pages, page, tk,
                     T, k_top):
    pairs = [(lambda pg: c_hbm.at[layer, pg, :, pl.ds(0, 128)], lambda s, p: bufc.at[s, pl.ds(p * page, page)], sem_c),
             (lambda pg: c_hbm.at[layer, pg, :, pl.ds(128, 128)], lambda s, p: bufs.at[s, pl.ds(p * page, page)], sem_s)]
    slot = _gather_pipeline(pt_ref, n_pages, page, pairs)
    R = N_HEADS * T
    P = n_pages * page
    nch = P // STRIDE_CMP
    nsp = selw_ref.shape[1]
    wb = pastwin_ref.shape[0]
    qidx = _row_mod(R, T)
    qpos = P + qidx
    qn = nq_ref[...].reshape(R, HEAD_DIM)
    qr = nqr_ref[...].reshape(R, HEAD_DIM)
    new_mask = lax.broadcasted_iota(jnp.int32, (1, T), 1) <= qidx
    init = (jnp.full((R, 1), NEG, F32), jnp.zeros((R, 1), F32), jnp.zeros((R, HEAD_DIM), F32))

    kcv = _compress(lambda l: bufc[slot, pl.ds(l, nch, stride=STRIDE_CMP), :], nch, pe_ref, w_ref, g_ref)
    cmp_end = lax.broadcasted_iota(jnp.int32, (1, nch), 1) * STRIDE_CMP + (L_CMP - 1)
    mask_c = cmp_end <= qpos
    s = jnp.where(mask_c, _mm_nt(qn, kcv[:, 0:HEAD_DIM]), NEG)
    p = jnp.where(mask_c, jnp.exp(s - jnp.max(s, axis=-1, keepdims=True)), 0.0)
    p = p / jnp.maximum(jnp.sum(p, axis=-1, keepdims=True), 1e-30)
    o_c = _mm(p, kcv[:, HEAD_DIM:])
    hi, mid, lo = _split3(jnp.sum(p.reshape(N_HEADS, T, nch), axis=0))
    selw = selw_ref[...]
    imp = (jnp.dot(hi, selw, preferred_element_type=F32) + jnp.dot(mid, selw, preferred_element_type=F32)
           + jnp.dot(lo, selw, preferred_element_type=F32))

    ids = lax.broadcasted_iota(jnp.int32, (1, nsp), 1)
    qp = P + lax.broadcasted_iota(jnp.int32, (T, 1), 0)
    cur = qp // L_SEL
    forced = (ids == 0) | (ids == cur) | (ids == cur - 1)
    score = jnp.where(forced, jnp.inf, jnp.where(ids * L_SEL <= qp, imp, -jnp.inf))
    selm = _topk_mask(score, ids, k_top).astype(MXU_DTYPE)
    selm = jnp.concatenate([selm] * N_HEADS, axis=0)

    def sel_step(j, carry):
        m, l, acc = carry
        k0 = pl.multiple_of(j * tk, tk)
        blk = _mm_nt(selm, et_ref[pl.ds(k0, tk), :]) > 0.5
        return _softmax_update(_mm_nt(qr, bufs[slot, pl.ds(k0, tk), 0:HEAD_DIM]), blk, m, l, acc,
                               bufs[slot, pl.ds(k0, tk), HEAD_DIM:])

    m, l, acc = lax.fori_loop(0, P // tk, sel_step, init)
    blk = (_mm_nt(selm, et_ref[pl.ds(P, T), :]) > 0.5) & new_mask
    m, l, acc = _softmax_update(_mm_nt(qr, newrows_ref[:, 128:192]), blk, m, l, acc, newrows_ref[:, 192:256])
    o_s = acc / jnp.maximum(l, 1e-30)

    mask_w = lax.broadcasted_iota(jnp.int32, (1, wb), 1) > qidx + (wb - WINDOW)
    m, l, acc = _softmax_update(_mm_nt(qr, pastwin_ref[:, 0:HEAD_DIM]), mask_w, *init, pastwin_ref[:, HEAD_DIM:])
    m, l, acc = _softmax_update(_mm_nt(qr, newwin_ref[:, 0:HEAD_DIM]), new_mask, m, l, acc, newwin_ref[:, HEAD_DIM:])
    o_w = acc / jnp.maximum(l, 1e-30)

    gates = [jnp.concatenate([small_ref[:, S_GATE + 3 * h + j:S_GATE + 3 * h + j + 1] for h in range(N_HEADS)], axis=0)
             for j in range(3)]
    _heads_to_lanes(o_ref, gates[0] * o_c + gates[1] * o_s + gates[2] * o_w, T, HEAD_DIM)

    keep = winout_ref.shape[0] - T
    winout_ref[0:keep, :] = pastwin_ref[wb - keep:wb, :]
    winout_ref[keep:keep + T, :] = newwin_ref[...]


def _nsa_sample(page_table, nq, nqr, small, nsa_rows, win_rows, state_win, pe, wblk, gk, selw, et, cache, layer, B, T):
    n_pages = page_table.shape[1]
    page = cache.shape[2]
    P = n_pages * page
    wb = state_win.shape[2]
    assert T < STRIDE_CMP and P % STRIDE_CMP == 0 and P + T >= WINDOW and wb >= WINDOW - T
    tk = min(512, P)
    k_top = min(N_SELECT, -(-(P + T) // L_SEL))
    full = lambda a: pl.BlockSpec(a.shape, lambda b, pt: (0,) * a.ndim)
    head = pl.BlockSpec((N_HEADS, T, HEAD_DIM), lambda b, pt: (0, b, 0))
    grid_spec = pltpu.PrefetchScalarGridSpec(
        num_scalar_prefetch=1,
        grid=(B,),
        in_specs=[head, head,
                  pl.BlockSpec((T, 128), lambda b, pt: (b, 0)),
                  pl.BlockSpec((T, 256), lambda b, pt: (b, 0)),
                  pl.BlockSpec((T, 128), lambda b, pt: (b, 0)),
                  pl.BlockSpec((None, None, wb, 128), lambda b, pt: (layer, b, 0, 0)),
                  full(pe), full(wblk), full(gk), full(selw), full(et),
                  pl.BlockSpec(memory_space=pl.ANY)],
        out_specs=[pl.BlockSpec((T, N_HEADS * HEAD_DIM), lambda b, pt: (b, 0)),
                   pl.BlockSpec((None, WINDOW, 128), lambda b, pt: (b, 0, 0))],
        scratch_shapes=[pltpu.VMEM((2, P, 128), F32), pltpu.VMEM((2, P, 128), F32),
                        pltpu.SemaphoreType.DMA((2,)), pltpu.SemaphoreType.DMA((2,))])
    return pl.pallas_call(
        functools.partial(_nsa_sample_body, layer=layer, n_pages=n_pages, page=page, tk=tk, T=T, k_top=k_top),
        grid_spec=grid_spec,
        out_shape=[jax.ShapeDtypeStruct((B * T, N_HEADS * HEAD_DIM), F32),
                   jax.ShapeDtypeStruct((B, WINDOW, 128), F32)],
        compiler_params=_cp("arbitrary"),
    )(page_table, nq, nqr, small, nsa_rows, win_rows, state_win, pe, wblk, gk, selw, et, cache)


def _block_ones(n, g):
    i = np.arange(n)
    return jnp.asarray(i[:, None] // g == i[None, :] // g, MXU_DTYPE)


def _sel_tables(S, nch):
    n_cmp = (S - L_CMP) // STRIDE_CMP + 1
    n_sel = -(-S // L_SEL)
    nsp = -(-n_sel // 128) * 128
    cs = np.arange(nch)[:, None] * STRIDE_CMP
    ss = np.arange(nsp)[None, :] * L_SEL
    w = np.maximum(np.minimum(cs + L_CMP, ss + L_SEL) - np.maximum(cs, ss), 0) // STRIDE_CMP
    w = np.where((np.arange(nch)[:, None] < n_cmp) & (np.arange(nsp)[None, :] < n_sel), w, 0)
    s_pad = -(-S // 16) * 16
    et = np.arange(s_pad)[:, None] // L_SEL == np.arange(nsp)[None, :]
    return jnp.asarray(w, MXU_DTYPE), jnp.asarray(et, MXU_DTYPE), n_sel


def _row_tile(M, pref):
    return pref if M % pref == 0 else M


def kernel(x_prompt, x_sample, cache_nsa, cache_fox_kv, cache_fox_logf, cache_mla, state_nsa_win, cache_mem, page_table, mem_prompt, norm_mix, w_in, b_fox, nsa_q_norm, nsa_k_norm, nsa_pe, nsa_w_cmp, fox_q_norm, fox_k_norm, mla_q_norm, mla_kv_norm, mla_w_uk, mla_w_uv, mla_k_norm, w_branch, w_o, norm_mem, mem_in_norm, w_q_mem, w_kv_mem, mem_q_norm, mem_k_norm, w_o_mem, norm_mlp, w_up, w_down):
    B, T, D = x_prompt.shape
    Bs, Ts, _ = x_sample.shape
    depth = w_in.shape[0]
    n_pages = page_table.shape[1]
    page = cache_nsa.shape[2]
    P = n_pages * page
    n_mem = mem_prompt.shape[1]
    bf = lambda a: a.astype(MXU_DTYPE)
    ones = lambda n: jnp.ones((depth, n), F32)

    splits = np.cumsum([0, 512, 384, 24, 512, 128, 8, 768, KV_LORA, MLA_ROPE, N_BRANCH * D])
    col = lambda i: w_in[:, :, splits[i]:splits[i + 1]]
    mla_q = col(6).reshape(depth, D, N_HEADS, MLA_QK)
    small_cols = jnp.concatenate([col(2), col(5), col(8), jnp.zeros((depth, D, 128 - 24 - 8 - MLA_ROPE), F32)], axis=2)
    w_rest = bf(jnp.concatenate([col(0), col(1), col(3), col(4),
                                 mla_q[..., :MLA_NOPE].reshape(depth, D, N_HEADS * MLA_NOPE),
                                 mla_q[..., MLA_NOPE:].reshape(depth, D, N_HEADS * MLA_ROPE),
                                 col(7), small_cols], axis=2))
    w_gate = bf(col(9))
    gqn = jnp.tile(nsa_q_norm, (1, N_HEADS))
    gkv = jnp.concatenate([ones(128), nsa_k_norm[:, 1], ones(64), nsa_k_norm[:, 2], ones(64)], axis=1)
    mkv = jnp.asarray(np.repeat([0, 0, 1, 0, 1, 0], HEAD_DIM)[None, :], F32)
    gqf = jnp.tile(fox_q_norm, (1, N_HEADS))
    gfk = jnp.concatenate([fox_k_norm, ones(64)], axis=1)
    gmn = jnp.tile(mla_q_norm[:, :MLA_NOPE], (1, N_HEADS))
    gmr = jnp.tile(mla_q_norm[:, MLA_NOPE:], (1, N_HEADS))
    bfox = jnp.concatenate([jnp.zeros((depth, S_LOGF), F32), b_fox, jnp.zeros((depth, 128 - S_LOGF - N_HEADS), F32)], axis=1)
    g64 = _block_ones(512, HEAD_DIM)
    g32 = _block_ones(256, MLA_ROPE)
    gx = jnp.asarray(np.arange(256)[:, None] // MLA_ROPE == np.arange(512)[None, :] // HEAD_DIM, MXU_DTYPE)
    gxt = gx.T
    pe = jnp.concatenate([nsa_pe[:, 0], nsa_pe[:, 1]], axis=2)
    zero = jnp.zeros_like(nsa_w_cmp[:, 0])
    wblk = bf(jnp.concatenate([jnp.concatenate([nsa_w_cmp[:, 0], zero], axis=3),
                               jnp.concatenate([zero, nsa_w_cmp[:, 1]], axis=3)], axis=2))
    gkc = jnp.concatenate([nsa_k_norm[:, 0], ones(64)], axis=1)
    wuk = bf(mla_w_uk.reshape(depth, KV_LORA, N_HEADS * MLA_NOPE))
    wuv = bf(mla_w_uv.reshape(depth, KV_LORA, N_HEADS * HEAD_DIM))
    wukt = bf(jnp.transpose(mla_w_uk, (0, 2, 3, 1)))
    wuvh = bf(jnp.transpose(mla_w_uv, (0, 2, 1, 3)))
    gkn = jnp.tile(mla_k_norm[:, :MLA_NOPE], (1, N_HEADS))
    wb_, wo_ = bf(w_branch), bf(w_o)
    wqm, wkvm, wom = bf(w_q_mem), bf(w_kv_mem), bf(w_o_mem)
    wu, wd = bf(w_up), bf(w_down)

    tm_post_p = _row_tile(T, 256)
    tm_post_s = _row_tile(Bs * Ts, 256)
    tab_p = _rope_tables(jnp.arange(T).astype(F32))
    tab_s = jnp.tile(_rope_tables((P + jnp.arange(Ts)).astype(F32)), (tm_post_s // Ts, 1))
    selw_p, et_p, _ = _sel_tables(T, T // STRIDE_CMP)
    selw_s, et_s, _ = _sel_tables(P + Ts, P // STRIDE_CMP)

    xp = x_prompt.reshape(B * T, D)
    xs = x_sample.reshape(Bs * Ts, D)
    tm_p = _row_tile(B * T, 512)
    tm_s = _row_tile(Bs * Ts, 512)
    tq = _row_tile(T, 128)
    tkf = min(256, T)
    outs = [[] for _ in range(11)]
    r2 = lambda a: a[None, :]

    for l in range(depth):
        post_consts = [r2(gqn[l]), r2(gkv[l]), mkv, r2(gqf[l]), r2(gfk[l]), r2(gmn[l]), r2(gmr[l]),
                       r2(mla_kv_norm[l]), r2(bfox[l]), g64, gx, gxt, g32]
        cmp_consts = (pe[l], wblk[l], r2(gkc[l]))

        def mixers_in(x, tm, tab, Tg, tm_post):
            g = _norm_matmul(x, r2(norm_mix[l]), w_gate[l], tm)
            p = _norm_matmul(x, r2(norm_mix[l]), w_rest[l], tm)
            return (g,) + tuple(_post(p, tab, Tg, post_consts, tm_post))

        def tail(x, g, o_nsa, o_fox, o_mla, mem_kv, Bg, Tg, tm):
            x = _merge(x, o_nsa, o_fox, o_mla, g, wb_[l], wo_[l], tm)
            x = _mem_attn(x, mem_kv, r2(norm_mem[l]), wqm[l], r2(mem_q_norm[l]), wom[l], Bg, Tg, _row_tile(Tg, 256))
            return _mlp(x, r2(norm_mlp[l]), wu[l], wd[l], tm, min(1024, wu.shape[2]))

        mem_p = _mem_kv_rows(mem_prompt.reshape(B * n_mem, D), r2(mem_in_norm[l]), wkvm[l], r2(mem_k_norm[l]),
                             _row_tile(B * n_mem, 512)).reshape(B, n_mem, 2 * MEM_W)
        g, nq, nqr, fq, mq, nsa_rows, win_rows, fox_rows, mla_rows, small = mixers_in(xp, tm_p, tab_p, T, tm_post_p)
        kcv = _nsa_compress(nsa_rows, B, T, *cmp_consts)
        o_nsa = _nsa_prompt(nq, nqr, small, nsa_rows, win_rows, kcv, selw_p, et_p, B, T, tq)
        o_fox = _flash_prompt(fq, fox_rows, None, _fox_cumsum(small, B, T), B, T, tq, tkf, scale=1.0)
        kk, vv = _mla_kv(mla_rows, B, T, wuk[l], wuv[l], r2(gkn[l]), r2(mla_k_norm[l, MLA_NOPE:]), g64, tkf)
        o_mla = _flash_prompt(mq, kk, vv, None, B, T, tq, tkf, scale=MLA_QK ** -0.5)
        xp = tail(xp, g, o_nsa, o_fox, o_mla, mem_p, B, T, tm_p)
        wkeep = min(WINDOW, T)
        for i, a in enumerate((nsa_rows.reshape(B, T, -1), fox_rows.reshape(B, T, -1),
                               small[:, S_LOGF:S_LOGF + N_HEADS].reshape(B, T, -1), mla_rows.reshape(B, T, -1),
                               win_rows.reshape(B, T, -1)[:, T - wkeep:], mem_p)):
            outs[i].append(a)

        g, nq, nqr, fq, mq, nsa_rows, win_rows, fox_rows, mla_rows, small = mixers_in(xs, tm_s, tab_s, Ts, tm_post_s)
        o_nsa, new_win = _nsa_sample(page_table, nq, nqr, small, nsa_rows, win_rows, state_nsa_win, *cmp_consts,
                                     selw_s, et_s, cache_nsa, l, Bs, Ts)
        o_fox = _fox_sample(page_table, fq, fox_rows, small, cache_fox_kv, cache_fox_logf, l, Bs, Ts)
        o_mla = _mla_sample(page_table, mq, mla_rows, wukt[l], wuvh[l], r2(mla_k_norm[l, :MLA_NOPE]),
                            r2(mla_k_norm[l, MLA_NOPE:]), cache_mla, l, Bs, Ts)
        xs = tail(xs, g, o_nsa, o_fox, o_mla, cache_mem[l], Bs, Ts, tm_s)
        for i, a in enumerate((nsa_rows.reshape(Bs, Ts, -1), fox_rows.reshape(Bs, Ts, -1),
                               small[:, S_LOGF:S_LOGF + N_HEADS].reshape(Bs, Ts, -1), mla_rows.reshape(Bs, Ts, -1),
                               new_win)):
            outs[6 + i].append(a)

    return (xp.reshape(B, T, D), xs.reshape(Bs, Ts, D)) + tuple(jnp.stack(o) for o in outs)
```

```python
import functools
import math

import jax
import jax.numpy as jnp
import numpy as np
from jax import lax
from jax.experimental import pallas as pl
from jax.experimental.pallas import tpu as pltpu

F32 = jnp.float32
MXU_DTYPE = jnp.bfloat16

HEAD_DIM = 64
N_HEADS = 8
MLA_NOPE = 64
MLA_ROPE = 32
MLA_QK = MLA_NOPE + MLA_ROPE
KV_LORA = 256
N_BRANCH = 3
L_CMP = 32
STRIDE_CMP = 16
L_SEL = 64
N_SELECT = 16
WINDOW = 512
N_MEM_HEADS = 4
MEM_HD = 128
MEM_W = N_MEM_HEADS * MEM_HD
ROPE_THETA = 10000.0
EPS = 1e-6
NEG = -1e30

P_QN, P_KVN, P_QF, P_KVF, P_QMN, P_QMR, P_CKV, P_SMALL, P_W = 0, 512, 896, 1408, 1536, 2048, 2304, 2560, 2688
S_GATE, S_LOGF, S_KR = 0, 24, 32

VMEM_LIMIT = 56 * 1024 * 1024


def _cp(*sem):
    return pltpu.CompilerParams(dimension_semantics=sem, vmem_limit_bytes=VMEM_LIMIT)


def _mm(a, b):
    return jnp.dot(a.astype(MXU_DTYPE), b.astype(MXU_DTYPE), preferred_element_type=F32)


def _mm_nt(a, b):
    return lax.dot_general(a.astype(MXU_DTYPE), b.astype(MXU_DTYPE), (((1,), (1,)), ((), ())),
                           preferred_element_type=F32)


def _split3(x):
    hi = x.astype(MXU_DTYPE)
    r = x - hi.astype(F32)
    mid = r.astype(MXU_DTYPE)
    lo = (r - mid.astype(F32)).astype(MXU_DTYPE)
    return hi, mid, lo


def _mm3(a, b_exact):
    b = b_exact.astype(MXU_DTYPE)
    return sum(jnp.dot(p, b, preferred_element_type=F32) for p in _split3(a))


def _mm3_nt_lhs_exact(a_exact, b):
    a = a_exact.astype(MXU_DTYPE)
    dn = (((1,), (1,)), ((), ()))
    return sum(lax.dot_general(a, p, dn, preferred_element_type=F32) for p in _split3(b))


def _rms(x, g):
    return x * lax.rsqrt(jnp.mean(x * x, axis=-1, keepdims=True) + EPS) * g


def _mm3_lhs_exact(a_exact, b):
    a = a_exact.astype(MXU_DTYPE)
    return sum(jnp.dot(a, p, preferred_element_type=F32) for p in _split3(b))


def _softmax_update(s, mask, m_old, l_old, acc_old, v):
    if mask is not None:
        s = jnp.where(mask, s, NEG)
    m_new = jnp.maximum(m_old, jnp.max(s, axis=-1, keepdims=True))
    alpha = jnp.exp(m_old - m_new)
    p = jnp.exp(s - m_new)
    if mask is not None:
        p = jnp.where(mask, p, 0.0)
    l_new = alpha * l_old + jnp.sum(p, axis=-1, keepdims=True)
    acc_new = alpha * acc_old + (v(p) if callable(v) else _mm(p, v))
    return m_new, l_new, acc_new


def _masked_softmax(s, mask):
    s = jnp.where(mask, s, NEG)
    p = jnp.where(mask, jnp.exp(s - jnp.max(s, axis=-1, keepdims=True)), 0.0)
    return p / jnp.maximum(jnp.sum(p, axis=-1, keepdims=True), 1e-30)


def _stack_rows(x3):
    return x3.reshape(x3.shape[0] * x3.shape[1], x3.shape[2])


def _topk_mask(score, ids, k_top):
    sel = jnp.zeros(score.shape, jnp.bool_)
    for _ in range(k_top):
        mx = jnp.max(score, axis=-1, keepdims=True)
        idx = jnp.min(jnp.where(score == mx, ids, 1 << 30), axis=-1, keepdims=True)
        pick = ids == idx
        sel = sel | (pick & (mx > -jnp.inf))
        score = jnp.where(pick, -jnp.inf, score)
    return sel


def _rope_tab_body(pos_ref, inv_ref, o_ref):
    ang = pos_ref[...] * inv_ref[...]
    c = jnp.cos(ang)
    s = jnp.sin(ang)
    lane = lax.broadcasted_iota(jnp.int32, (ang.shape[0], 128), 1)
    first32 = (lane % 64) < 32
    first16 = (lane % 32) < 16
    c32, s32 = c[:, :128], s[:, :128]
    c16, s16 = c[:, 128:], s[:, 128:]
    o_ref[:, 0:128] = c32
    o_ref[:, 128:256] = jnp.where(first32, -s32, 0.0)
    o_ref[:, 256:384] = jnp.where(first32, 0.0, s32)
    o_ref[:, 384:512] = c16
    o_ref[:, 512:640] = jnp.where(first16, -s16, 0.0)
    o_ref[:, 640:768] = jnp.where(first16, 0.0, s16)


def _rope_tables(pos):
    T = pos.shape[0]
    inv32 = ROPE_THETA ** (-jnp.arange(32, dtype=F32) / 32)
    inv16 = ROPE_THETA ** (-jnp.arange(16, dtype=F32) / 16)
    inv = jnp.concatenate([jnp.tile(inv32, 4), jnp.tile(inv16, 8)])[None, :]
    return pl.pallas_call(
        _rope_tab_body,
        name="rope_tables",
        out_shape=jax.ShapeDtypeStruct((T, 768), F32),
    )(pos[:, None], inv)


def _norm_mm_body(x_ref, g_ref, w_ref, o_ref):
    o_ref[...] = _mm(_rms(x_ref[...], g_ref[...]), w_ref[...])


def _norm_matmul(x, g, w, tm):
    M, K = x.shape
    N = w.shape[1]
    return pl.pallas_call(
        _norm_mm_body,
        name="norm_matmul",
        grid=(M // tm,),
        in_specs=[pl.BlockSpec((tm, K), lambda i: (i, 0)),
                  pl.BlockSpec((1, K), lambda i: (0, 0)),
                  pl.BlockSpec((K, N), lambda i: (0, 0))],
        out_specs=pl.BlockSpec((tm, N), lambda i: (i, 0)),
        out_shape=jax.ShapeDtypeStruct((M, N), F32),
        compiler_params=_cp("parallel"),
    )(x, g, w)


def _mem_kv_body(x_ref, g_ref, w_ref, gk_ref, o_ref):
    kv = _mm(_rms(x_ref[...], g_ref[...]), w_ref[...])
    for h in range(N_MEM_HEADS):
        sl = slice(h * MEM_HD, (h + 1) * MEM_HD)
        o_ref[:, sl] = _rms(kv[:, sl], gk_ref[...])
    o_ref[:, MEM_W:] = kv[:, MEM_W:]


def _mem_kv_rows(mem2d, g_in, w_kv, g_k, tm):
    M, K = mem2d.shape
    N = w_kv.shape[1]
    return pl.pallas_call(
        _mem_kv_body,
        name="mem_kv_rows",
        grid=(M // tm,),
        in_specs=[pl.BlockSpec((tm, K), lambda i: (i, 0)),
                  pl.BlockSpec((1, K), lambda i: (0, 0)),
                  pl.BlockSpec((K, N), lambda i: (0, 0)),
                  pl.BlockSpec((1, MEM_HD), lambda i: (0, 0))],
        out_specs=pl.BlockSpec((tm, N), lambda i: (i, 0)),
        out_shape=jax.ShapeDtypeStruct((M, N), F32),
        compiler_params=_cp("parallel"),
    )(mem2d, g_in, w_kv, g_k)


def _post_body(p_ref, tab_ref, gqn_ref, gkv_ref, mkv_ref, gqf_ref, gfk_ref, gmn_ref, gmr_ref, gckv_ref,
               bfox_ref, g64_ref, gx_ref, gxt_ref, g32_ref,
               nq_ref, nqr_ref, fq_ref, mq_ref, nsa_ref, win_ref, fox_ref, mla_ref, small_ref):
    tab = tab_ref[...]
    c32, sp32, sm32, c16, sp16, sm16 = [tab[:, i * 128:(i + 1) * 128] for i in range(6)]

    def tile(t, n):
        return t if n == 1 else jnp.concatenate([t] * n, axis=1)

    def rope(x, c, sp, sm, s):
        w = x.shape[1]
        n = w // 128
        return x * tile(c, n) + pltpu.roll(x, w - s, 1) * tile(sp, n) + pltpu.roll(x, s, 1) * tile(sm, n)

    g64 = g64_ref[...]
    scale = HEAD_DIM ** -0.5

    x = p_ref[:, P_QN:P_QN + 512]
    y = x * lax.rsqrt(_mm3(x * x, g64) / HEAD_DIM + EPS) * gqn_ref[...]
    yr = rope(y, c32, sp32, sm32, 32)
    xf = p_ref[:, P_QF:P_QF + 512]
    yf = xf * lax.rsqrt(_mm3(xf * xf, g64) / HEAD_DIM + EPS) * gqf_ref[...]
    for h in range(N_HEADS):
        sl = slice(h * HEAD_DIM, (h + 1) * HEAD_DIM)
        nq_ref[h] = (y[:, sl] * scale).astype(nq_ref.dtype)
        nqr_ref[h] = (yr[:, sl] * scale).astype(nqr_ref.dtype)
        fq_ref[h] = (yf[:, sl] * scale).astype(fq_ref.dtype)

    x = p_ref[:, P_KVN:P_KVN + 384]
    normed = mkv_ref[...] > 0.5
    yn = x * lax.rsqrt(_mm3(x * x, g64[:384, :384]) / HEAD_DIM + EPS) * gkv_ref[...]
    y = jnp.where(normed, rope(yn, c32, sp32, sm32, 32), x)
    nsa_ref[...] = y[:, :256]
    win_ref[...] = y[:, 256:]

    x = p_ref[:, P_KVF:P_KVF + 128]
    lane = lax.broadcasted_iota(jnp.int32, x.shape, 1)
    yn = x * lax.rsqrt(_mm3(x * x, g64[:128, :128]) / HEAD_DIM + EPS) * gfk_ref[...]
    fox_ref[...] = jnp.where(lane < HEAD_DIM, yn, x)

    xn = p_ref[:, P_QMN:P_QMN + 512]
    xr = rope(p_ref[:, P_QMR:P_QMR + 256], c16, sp16, sm16, 16)
    xn2, xr2 = xn * xn, xr * xr
    ms_n = (_mm3(xn2, g64) + _mm3(xr2, gx_ref[...])) / MLA_QK
    ms_r = (_mm3(xn2, gxt_ref[...]) + _mm3(xr2, g32_ref[...])) / MLA_QK
    yn = xn * lax.rsqrt(ms_n + EPS) * gmn_ref[...]
    yr = xr * lax.rsqrt(ms_r + EPS) * gmr_ref[...]
    for h in range(N_HEADS):
        mq_ref[h, :, 0:MLA_NOPE] = yn[:, h * MLA_NOPE:(h + 1) * MLA_NOPE].astype(mq_ref.dtype)
        mq_ref[h, :, MLA_NOPE:MLA_QK] = yr[:, h * MLA_ROPE:(h + 1) * MLA_ROPE].astype(mq_ref.dtype)

    x = p_ref[:, P_SMALL:P_SMALL + 128]
    sig = jax.nn.sigmoid(x)
    lf = jax.nn.log_sigmoid(x + bfox_ref[...])
    rr = rope(x, c16, sp16, sm16, 16)
    small = jnp.where(lane < S_LOGF, sig, jnp.where(lane < S_KR, lf, jnp.where(lane < S_KR + MLA_ROPE, rr, 0.0)))
    small_ref[...] = small

    x = p_ref[:, P_CKV:P_CKV + KV_LORA]
    mla_ref[:, 0:KV_LORA] = _rms(x, gckv_ref[...])
    mla_ref[:, KV_LORA:KV_LORA + MLA_ROPE] = small[:, S_KR:S_KR + MLA_ROPE]


def _post(p, tab, T, consts, tm):
    M = p.shape[0]
    if T >= tm:
        nt = T // tm
        tab_map = lambda i: (i % nt, 0)
    else:
        tab_map = lambda i: (0, 0)
    row = lambda w: pl.BlockSpec((tm, w), lambda i: (i, 0))
    head = lambda d: pl.BlockSpec((N_HEADS, tm, d), lambda i: (0, i, 0))
    full = lambda a: pl.BlockSpec(a.shape, lambda i: (0,) * a.ndim)
    qd = MXU_DTYPE
    out_shape = [jax.ShapeDtypeStruct((N_HEADS, M, HEAD_DIM), qd)] * 3 + [
        jax.ShapeDtypeStruct((N_HEADS, M, MLA_QK), qd),
        jax.ShapeDtypeStruct((M, 256), F32), jax.ShapeDtypeStruct((M, 128), F32),
        jax.ShapeDtypeStruct((M, 128), F32), jax.ShapeDtypeStruct((M, KV_LORA + MLA_ROPE), F32),
        jax.ShapeDtypeStruct((M, 128), F32)]
    out_specs = [head(HEAD_DIM)] * 3 + [head(MLA_QK), row(256), row(128), row(128), row(KV_LORA + MLA_ROPE), row(128)]
    return pl.pallas_call(
        _post_body,
        name="post",
        grid=(M // tm,),
        in_specs=[row(P_W), pl.BlockSpec((tm, 768), tab_map)] + [full(c) for c in consts],
        out_specs=out_specs,
        out_shape=out_shape,
        compiler_params=_cp("parallel"),
    )(p, tab, *consts)


def _compress(load_xl, nch, pe_ref, w_ref, g_ref):
    acc_a = jnp.zeros((nch, 128), F32)
    acc_b = jnp.zeros((nch, 128), F32)
    for l in range(STRIDE_CMP):
        xl = load_xl(l)
        acc_a = acc_a + _mm(xl + pe_ref[l:l + 1, :], w_ref[l])
        acc_b = acc_b + _mm(xl + pe_ref[STRIDE_CMP + l:STRIDE_CMP + l + 1, :], w_ref[STRIDE_CMP + l])
    kcv = acc_a + pltpu.roll(acc_b, nch - 1, 0)
    lane = lax.broadcasted_iota(jnp.int32, kcv.shape, 1)
    is_k = lane < HEAD_DIM
    ms = jnp.sum(jnp.where(is_k, kcv * kcv, 0.0), axis=-1, keepdims=True) / HEAD_DIM
    return jnp.where(is_k, kcv * lax.rsqrt(ms + EPS) * g_ref[...], kcv)


def _cmp_body(rows_ref, pe_ref, w_ref, g_ref, o_ref, *, nch):
    o_ref[...] = _compress(lambda l: rows_ref[pl.ds(l, nch, stride=STRIDE_CMP), :], nch, pe_ref, w_ref, g_ref)


def _nsa_compress(nsa_rows, B, S, pe, wblk, gk):
    nch = S // STRIDE_CMP
    return pl.pallas_call(
        functools.partial(_cmp_body, nch=nch),
        name="nsa_compress",
        grid=(B,),
        in_specs=[pl.BlockSpec((S, 128), lambda b: (b, 0)),
                  pl.BlockSpec(pe.shape, lambda b: (0, 0)),
                  pl.BlockSpec(wblk.shape, lambda b: (0, 0, 0)),
                  pl.BlockSpec((1, 128), lambda b: (0, 0))],
        out_specs=pl.BlockSpec((None, nch, 128), lambda b: (b, 0, 0)),
        out_shape=jax.ShapeDtypeStruct((B, nch, 128), F32),
        compiler_params=_cp("parallel"),
    )(nsa_rows, pe, wblk, gk)


def _nsa_prompt_body(nq_ref, nqr_ref, small_ref, rows_ref, win_ref, kcv_ref, selw_ref, et_ref, o_ref,
                     m_ref, l_ref, acc_ref, *, tq, tk, wlen, k_top):
    H = N_HEADS
    T = rows_ref.shape[0]
    qi = pl.program_id(1)
    q0 = qi * tq
    qpos = q0 + lax.broadcasted_iota(jnp.int32, (tq, 1), 0)
    nch = kcv_ref.shape[0]
    nsp = selw_ref.shape[1]
    gates = [jnp.concatenate([small_ref[:, S_GATE + 3 * h + j:S_GATE + 3 * h + j + 1] for h in range(H)],
                             axis=0).reshape(H, tq, 1) for j in range(3)]

    kc = kcv_ref[:, 0:HEAD_DIM]
    vc = kcv_ref[:, HEAD_DIM:]
    cmp_end = lax.broadcasted_iota(jnp.int32, (1, nch), 1) * STRIDE_CMP + (L_CMP - 1)
    mask_c = (cmp_end <= qpos)[None]
    p = _masked_softmax(_mm_nt(_stack_rows(nq_ref[...]), kc).reshape(H, tq, nch), mask_c)
    out = gates[0] * _mm(_stack_rows(p), vc).reshape(H, tq, HEAD_DIM)
    imp = _mm3(jnp.sum(p, axis=0), selw_ref[...])

    ids = lax.broadcasted_iota(jnp.int32, (1, nsp), 1)
    cur = qpos // L_SEL
    forced = (ids == 0) | (ids == cur) | (ids == cur - 1)
    avail = ids * L_SEL <= qpos
    score = jnp.where(forced, jnp.inf, jnp.where(avail, imp, -jnp.inf))
    selm = _topk_mask(score, ids, k_top).astype(MXU_DTYPE)
    qr = _stack_rows(nqr_ref[...])

    m_ref[...] = jnp.full(m_ref.shape, NEG, F32)
    l_ref[...] = jnp.zeros(l_ref.shape, F32)
    acc_ref[...] = jnp.zeros(acc_ref.shape, F32)

    def sel_step(kt, carry):
        k0 = pl.multiple_of(kt * tk, tk)
        blk = _mm_nt(selm, et_ref[pl.ds(k0, tk), :]) > 0.5
        kpos = k0 + lax.broadcasted_iota(jnp.int32, (1, tk), 1)
        vv = rows_ref[pl.ds(k0, tk), 192:256]
        s3 = _mm_nt(qr, rows_ref[pl.ds(k0, tk), 128:192]).reshape(H, tq, tk)
        m, l, a = _softmax_update(s3, (blk & (kpos <= qpos))[None], m_ref[...], l_ref[...], acc_ref[...],
                                  lambda w: _mm(_stack_rows(w), vv).reshape(H, tq, HEAD_DIM))
        m_ref[...] = m
        l_ref[...] = l
        acc_ref[...] = a
        return carry

    lax.fori_loop(0, (q0 + tq - 1) // tk + 1, sel_step, 0)
    out = out + gates[1] * (acc_ref[...] / jnp.maximum(l_ref[...], 1e-30))

    w0 = pl.multiple_of(jnp.clip(q0 - WINDOW, 0, T - wlen), 8)
    kpos = w0 + lax.broadcasted_iota(jnp.int32, (1, wlen), 1)
    mask_w = ((kpos > qpos - WINDOW) & (kpos <= qpos))[None]
    p = _masked_softmax(_mm_nt(qr, win_ref[pl.ds(w0, wlen), 0:HEAD_DIM]).reshape(H, tq, wlen), mask_w)
    out = out + gates[2] * _mm(_stack_rows(p), win_ref[pl.ds(w0, wlen), HEAD_DIM:]).reshape(H, tq, HEAD_DIM)

    for h in range(H):
        o_ref[:, h * HEAD_DIM:(h + 1) * HEAD_DIM] = out[h]


def _nsa_prompt(nq, nqr, small, nsa_rows, win_rows, kcv, selw, et, B, T, tq):
    nqt = T // tq
    tk = min(512, T)
    wlen = min(T, WINDOW + tq)
    assert (T - wlen) % 8 == 0 and T % tk == 0
    n_sel = -(-T // L_SEL)
    k_top = min(N_SELECT, n_sel)
    nch = kcv.shape[1]
    head = pl.BlockSpec((N_HEADS, tq, HEAD_DIM), lambda b, i: (0, b * nqt + i, 0))
    return pl.pallas_call(
        functools.partial(_nsa_prompt_body, tq=tq, tk=tk, wlen=wlen, k_top=k_top),
        grid=(B, nqt),
        in_specs=[head, head,
                  pl.BlockSpec((tq, 128), lambda b, i: (b * nqt + i, 0)),
                  pl.BlockSpec((T, 256), lambda b, i: (b, 0)),
                  pl.BlockSpec((T, 128), lambda b, i: (b, 0)),
                  pl.BlockSpec((None, nch, 128), lambda b, i: (b, 0, 0)),
                  pl.BlockSpec(selw.shape, lambda b, i: (0, 0)),
                  pl.BlockSpec(et.shape, lambda b, i: (0, 0))],
        out_specs=pl.BlockSpec((tq, N_HEADS * HEAD_DIM), lambda b, i: (b * nqt + i, 0)),
        out_shape=jax.ShapeDtypeStruct((B * T, N_HEADS * HEAD_DIM), F32),
        scratch_shapes=[pltpu.VMEM((N_HEADS, tq, 1), F32), pltpu.VMEM((N_HEADS, tq, 1), F32),
                        pltpu.VMEM((N_HEADS, tq, HEAD_DIM), F32)],
        compiler_params=_cp("parallel", "arbitrary"),
        name="nsa_prompt",
    )(nq, nqr, small, nsa_rows, win_rows, kcv, selw, et)


def _cumsum_body(small_ref, cst_ref, *, S, tc):
    r = lax.broadcasted_iota(jnp.int32, (tc, tc), 0)
    c = lax.broadcasted_iota(jnp.int32, (tc, tc), 1)
    upper = (r <= c).astype(MXU_DTYPE)

    def step(i, carry):
        k0 = pl.multiple_of(i * tc, tc)
        xt = small_ref[pl.ds(k0, tc), :].T
        ct = _mm3(xt, upper) + carry
        cst_ref[i] = ct[S_LOGF:S_LOGF + N_HEADS, :]
        return ct[:, tc - 1:tc]

    lax.fori_loop(0, S // tc, step, jnp.zeros((128, 1), F32))


def _fox_cumsum(small, B, S, tc):
    return pl.pallas_call(
        functools.partial(_cumsum_body, S=S, tc=tc),
        name="forget_cumsum",
        grid=(B,),
        in_specs=[pl.BlockSpec((S, 128), lambda b: (b, 0))],
        out_specs=pl.BlockSpec((None, S // tc, N_HEADS, tc), lambda b: (b, 0, 0, 0)),
        out_shape=jax.ShapeDtypeStruct((B, S // tc, N_HEADS, tc), F32),
        compiler_params=_cp("parallel"),
    )(small)


def _flash_body(*refs, tq, tk, dv, per_head, decay, scale):
    if decay:
        q_ref, k_ref, c_ref, o_ref, m_ref, l_ref, acc_ref = refs
        v_ref = None
    elif per_head:
        q_ref, k_ref, v_ref, o_ref, m_ref, l_ref, acc_ref = refs
    else:
        q_ref, k_ref, o_ref, m_ref, l_ref, acc_ref = refs
        v_ref = None
    H = N_HEADS
    qi = pl.program_id(1)
    kj = pl.program_id(2)
    q0 = qi * tq
    k0 = kj * tk

    @pl.when(kj == 0)
    def _():
        m_ref[...] = jnp.full(m_ref.shape, NEG, F32)
        l_ref[...] = jnp.zeros(l_ref.shape, F32)
        acc_ref[...] = jnp.zeros(acc_ref.shape, F32)

    @pl.when(k0 <= q0 + tq - 1)
    def _():
        qpos = q0 + lax.broadcasted_iota(jnp.int32, (tq, 1), 0)
        kpos = k0 + lax.broadcasted_iota(jnp.int32, (1, tk), 1)
        mask = (kpos <= qpos)[None]
        q = q_ref[...]
        if per_head:
            s3 = lax.dot_general(q, k_ref[...], (((2,), (2,)), ((0,), (0,))), preferred_element_type=F32)
            pv = lambda w: lax.dot_general(w.astype(MXU_DTYPE), v_ref[...], (((2,), (1,)), ((0,), (0,))),
                                           preferred_element_type=F32)
        else:
            vv = k_ref[:, HEAD_DIM:2 * HEAD_DIM]
            s3 = _mm_nt(_stack_rows(q), k_ref[:, 0:HEAD_DIM]).reshape(H, tq, tk)
            pv = lambda w: _mm(_stack_rows(w), vv).reshape(H, tq, dv)
        if scale != 1.0:
            s3 = s3 * scale
        if decay:
            s3 = s3 - c_ref[...][:, None, :]
        m, l, a = _softmax_update(s3, mask, m_ref[...], l_ref[...], acc_ref[...], pv)
        m_ref[...] = m
        l_ref[...] = l
        acc_ref[...] = a

    @pl.when(kj == pl.num_programs(2) - 1)
    def _():
        o = acc_ref[...] / jnp.maximum(l_ref[...], 1e-30)
        for h in range(H):
            o_ref[:, h * dv:(h + 1) * dv] = o[h]


def _flash_prompt(q, k, v, cst, B, T, tq, tk, *, scale):
    nqt, nkt = T // tq, T // tk
    d = q.shape[-1]
    per_head = v is not None
    decay = cst is not None
    last = lambda i: (i * tq + tq - 1) // tk
    in_specs = [pl.BlockSpec((N_HEADS, tq, d), lambda b, i, j: (0, b * nqt + i, 0))]
    args = [q]
    if per_head:
        dv = v.shape[-1]
        in_specs += [pl.BlockSpec((None, N_HEADS, tk, d), lambda b, i, j: (b, 0, jnp.minimum(j, last(i)), 0)),
                     pl.BlockSpec((None, N_HEADS, tk, dv), lambda b, i, j: (b, 0, jnp.minimum(j, last(i)), 0))]
        args += [k, v]
    else:
        dv = HEAD_DIM
        in_specs += [pl.BlockSpec((tk, 128), lambda b, i, j: (b * nkt + jnp.minimum(j, last(i)), 0))]
        args += [k]
    if decay:
        assert cst.shape[-1] == tk
        in_specs += [pl.BlockSpec((None, None, N_HEADS, tk), lambda b, i, j: (b, jnp.minimum(j, last(i)), 0, 0))]
        args += [cst]
    return pl.pallas_call(
        functools.partial(_flash_body, tq=tq, tk=tk, dv=dv, per_head=per_head, decay=decay, scale=scale),
        grid=(B, nqt, nkt),
        in_specs=in_specs,
        out_specs=pl.BlockSpec((tq, N_HEADS * dv), lambda b, i, j: (b * nqt + i, 0)),
        out_shape=jax.ShapeDtypeStruct((B * T, N_HEADS * dv), F32),
        scratch_shapes=[pltpu.VMEM((N_HEADS, tq, 1), F32), pltpu.VMEM((N_HEADS, tq, 1), F32),
                        pltpu.VMEM((N_HEADS, tq, dv), F32)],
        compiler_params=_cp("parallel", "parallel", "arbitrary"),
        name="flash_latent" if per_head else "flash_forget",
    )(*args)


def _mla_kv_body(r_ref, wuk_ref, wuv_ref, gkn_ref, gkr_ref, g64_ref, k_ref, v_ref):
    ckv = r_ref[:, 0:KV_LORA]
    kr = r_ref[:, KV_LORA:KV_LORA + MLA_ROPE]
    kn = _mm(ckv, wuk_ref[...])
    ss = _mm3(kn * kn, g64_ref[...]) + jnp.sum(kr * kr, axis=-1, keepdims=True)
    inv = lax.rsqrt(ss / MLA_QK + EPS)
    knn = kn * gkn_ref[...] * inv
    krg = kr * gkr_ref[...]
    vv = _mm(ckv, wuv_ref[...])
    for h in range(N_HEADS):
        sl = slice(h * HEAD_DIM, (h + 1) * HEAD_DIM)
        k_ref[h, :, 0:MLA_NOPE] = knn[:, sl].astype(k_ref.dtype)
        k_ref[h, :, MLA_NOPE:MLA_QK] = (krg * inv[:, h * HEAD_DIM:h * HEAD_DIM + MLA_ROPE]).astype(k_ref.dtype)
        v_ref[h] = vv[:, sl].astype(v_ref.dtype)


def _mla_kv(mla_rows, B, T, wuk, wuv, gkn, gkr, g64, tk):
    nkt = T // tk
    full = lambda a: pl.BlockSpec(a.shape, lambda b, j: (0,) * a.ndim)
    return pl.pallas_call(
        _mla_kv_body,
        name="latent_kv",
        grid=(B, nkt),
        in_specs=[pl.BlockSpec((tk, KV_LORA + MLA_ROPE), lambda b, j: (b * nkt + j, 0)),
                  full(wuk), full(wuv), full(gkn), full(gkr), full(g64)],
        out_specs=[pl.BlockSpec((None, N_HEADS, tk, MLA_QK), lambda b, j: (b, 0, j, 0)),
                   pl.BlockSpec((None, N_HEADS, tk, HEAD_DIM), lambda b, j: (b, 0, j, 0))],
        out_shape=[jax.ShapeDtypeStruct((B, N_HEADS, T, MLA_QK), MXU_DTYPE),
                   jax.ShapeDtypeStruct((B, N_HEADS, T, HEAD_DIM), MXU_DTYPE)],
        compiler_params=_cp("parallel", "parallel"),
    )(mla_rows, wuk, wuv, gkn, gkr, g64)


def _merge_body(x_ref, o1_ref, o2_ref, o3_ref, g_ref, wb_ref, wo_ref, out_ref):
    d = x_ref.shape[1]
    acc = jnp.zeros(x_ref.shape, F32)
    for b, o_ref in enumerate((o1_ref, o2_ref, o3_ref)):
        acc = acc + jax.nn.sigmoid(g_ref[:, b * d:(b + 1) * d]) * _mm(o_ref[...], wb_ref[b])
    out_ref[...] = x_ref[...] + _mm(acc, wo_ref[...])


def _merge(x, o1, o2, o3, g, wb, wo, tm):
    M, D = x.shape
    row = lambda w: pl.BlockSpec((tm, w), lambda i: (i, 0))
    return pl.pallas_call(
        _merge_body,
        name="merge",
        grid=(M // tm,),
        in_specs=[row(D), row(o1.shape[1]), row(o2.shape[1]), row(o3.shape[1]), row(g.shape[1]),
                  pl.BlockSpec(wb.shape, lambda i: (0, 0, 0)), pl.BlockSpec(wo.shape, lambda i: (0, 0))],
        out_specs=row(D),
        out_shape=jax.ShapeDtypeStruct((M, D), F32),
        compiler_params=_cp("parallel"),
    )(x, o1, o2, o3, g, wb, wo)


def _mem_attn_body(x_ref, kv_ref, g_ref, wq_ref, gq_ref, wo_ref, o_ref):
    x = x_ref[...]
    q = _mm(_rms(x, g_ref[...]), wq_ref[...])
    outs = []
    for h in range(N_MEM_HEADS):
        sl = slice(h * MEM_HD, (h + 1) * MEM_HD)
        qh = _rms(q[:, sl], gq_ref[...])
        s = _mm_nt(qh, kv_ref[:, sl]) * MEM_HD ** -0.5
        p = jnp.exp(s - jnp.max(s, axis=-1, keepdims=True))
        p = p / jnp.sum(p, axis=-1, keepdims=True)
        outs.append(_mm(p, kv_ref[:, MEM_W + h * MEM_HD:MEM_W + (h + 1) * MEM_HD]))
    o_ref[...] = x + _mm(jnp.concatenate(outs, axis=1), wo_ref[...])


def _mem_attn(x, mem_kv, g, wq, gq, wo, B, T, tm):
    D = x.shape[1]
    nt = T // tm
    n_mem = mem_kv.shape[1]
    full = lambda a: pl.BlockSpec(a.shape, lambda b, i: (0,) * a.ndim)
    return pl.pallas_call(
        _mem_attn_body,
        name="mem_attn",
        grid=(B, nt),
        in_specs=[pl.BlockSpec((tm, D), lambda b, i: (b * nt + i, 0)),
                  pl.BlockSpec((None, n_mem, 2 * MEM_W), lambda b, i: (b, 0, 0)),
                  full(g), full(wq), full(gq), full(wo)],
        out_specs=pl.BlockSpec((tm, D), lambda b, i: (b * nt + i, 0)),
        out_shape=jax.ShapeDtypeStruct(x.shape, F32),
        compiler_params=_cp("parallel", "parallel"),
    )(x, mem_kv, g, wq, gq, wo)


def _mlp_body(x_ref, g_ref, wu_ref, wd_ref, o_ref, h_ref, acc_ref):
    j = pl.program_id(1)

    @pl.when(j == 0)
    def _():
        h_ref[...] = _rms(x_ref[...], g_ref[...]).astype(h_ref.dtype)
        acc_ref[...] = jnp.zeros(acc_ref.shape, F32)

    u = jnp.dot(h_ref[...], wu_ref[...], preferred_element_type=F32)
    acc_ref[...] += _mm(jnp.square(jnp.maximum(u, 0.0)), wd_ref[...])

    @pl.when(j == pl.num_programs(1) - 1)
    def _():
        o_ref[...] = x_ref[...] + acc_ref[...]


def _mlp(x, g, wu, wd, tm, tf):
    M, D = x.shape
    FF = wu.shape[1]
    return pl.pallas_call(
        _mlp_body,
        name="mlp",
        grid=(M // tm, FF // tf),
        in_specs=[pl.BlockSpec((tm, D), lambda i, j: (i, 0)),
                  pl.BlockSpec((1, D), lambda i, j: (0, 0)),
                  pl.BlockSpec((D, tf), lambda i, j: (0, j)),
                  pl.BlockSpec((tf, D), lambda i, j: (j, 0))],
        out_specs=pl.BlockSpec((tm, D), lambda i, j: (i, 0)),
        out_shape=jax.ShapeDtypeStruct((M, D), F32),
        scratch_shapes=[pltpu.VMEM((tm, D), MXU_DTYPE), pltpu.VMEM((tm, D), F32)],
        compiler_params=_cp("parallel", "arbitrary"),
    )(x, g, wu, wd)


def _page_copies(pt_ref, bb, slot, n_pages, page, pairs, start):
    def body(p, carry):
        pg = pt_ref[bb, p]
        for src_fn, dst_fn, sem in pairs:
            cp = pltpu.make_async_copy(src_fn(pg), dst_fn(slot, p), sem.at[slot])
            if start:
                cp.start()
            else:
                cp.wait()
        return carry

    lax.fori_loop(0, n_pages, body, 0)


def _gather_pipeline(pt_ref, n_pages, page, pairs):
    b = pl.program_id(0)
    nb = pl.num_programs(0)

    @pl.when(b == 0)
    def _():
        _page_copies(pt_ref, 0, 0, n_pages, page, pairs, True)

    @pl.when(b + 1 < nb)
    def _():
        _page_copies(pt_ref, b + 1, (b + 1) % 2, n_pages, page, pairs, True)

    slot = b % 2
    _page_copies(pt_ref, b, slot, n_pages, page, pairs, False)
    return slot


def _row_mod(rows, T):
    return lax.broadcasted_iota(jnp.int32, (rows, 1), 0) % T


def _heads_to_lanes(o_ref, o, T, dv):
    for h in range(N_HEADS):
        o_ref[:, h * dv:(h + 1) * dv] = o[h * T:(h + 1) * T, :]


def _fox_sample_body(pt_ref, q_ref, newkv_ref, newsmall_ref, lstrict_ref, kv_hbm, lf_hbm, o_ref, kvbuf, lfbuf, ct_ref,
                     sem_kv, sem_lf, *, layer, n_pages, page, tk, T):
    pairs = [(lambda pg: kv_hbm.at[layer, pg], lambda s, p: kvbuf.at[s, pl.ds(p * page, page)], sem_kv),
             (lambda pg: lf_hbm.at[layer, pg], lambda s, p: lfbuf.at[s, p], sem_lf)]
    slot = _gather_pipeline(pt_ref, n_pages, page, pairs)
    H = N_HEADS
    R = H * T
    P = n_pages * page

    upper = (lax.broadcasted_iota(jnp.int32, (page, page), 0)
             <= lax.broadcasted_iota(jnp.int32, (page, page), 1)).astype(MXU_DTYPE)
    local = _mm3(lfbuf[slot].reshape(n_pages * H, page), upper)
    tot = jnp.broadcast_to(local[:, page - 1:page], local.shape)
    ct_all = local + _mm3_lhs_exact(lstrict_ref[...], tot)
    ct_ref[...] = ct_all.reshape(n_pages, H, page)
    total = ct_all[(n_pages - 1) * H:, page - 1:page]
    eye = (lax.broadcasted_iota(jnp.int32, (H, H), 0) == lax.broadcasted_iota(jnp.int32, (H, H), 1)).astype(MXU_DTYPE)
    upper_t = (lax.broadcasted_iota(jnp.int32, (T, T), 0) <= lax.broadcasted_iota(jnp.int32, (T, T), 1)).astype(MXU_DTYPE)
    ct_new = _mm3(_mm3_nt_lhs_exact(eye, newsmall_ref[:, S_LOGF:S_LOGF + H]), upper_t) + total

    q = _stack_rows(q_ref[...])
    ppt = tk // page

    def step(j, carry):
        m, l, acc = carry
        k0 = pl.multiple_of(j * tk, tk)
        kv = kvbuf[slot, pl.ds(k0, tk), :]
        ct = jnp.concatenate([ct_ref[j * ppt + i] for i in range(ppt)], axis=1)
        s = (_mm_nt(q, kv[:, 0:HEAD_DIM]).reshape(H, T, tk) - ct[:, None, :]).reshape(R, tk)
        return _softmax_update(s, None, m, l, acc, kv[:, HEAD_DIM:])

    init = (jnp.full((R, 1), NEG, F32), jnp.zeros((R, 1), F32), jnp.zeros((R, HEAD_DIM), F32))
    m, l, acc = lax.fori_loop(0, P // tk, step, init)

    knew = newkv_ref[...]
    s = (_mm_nt(q, knew[:, 0:HEAD_DIM]).reshape(H, T, T) - ct_new[:, None, :]).reshape(R, T)
    mask = lax.broadcasted_iota(jnp.int32, (1, T), 1) <= _row_mod(R, T)
    m, l, acc = _softmax_update(s, mask, m, l, acc, knew[:, HEAD_DIM:])
    _heads_to_lanes(o_ref, acc / jnp.maximum(l, 1e-30), T, HEAD_DIM)


def _fox_sample(page_table, fq, fox_rows, small, lstrict, cache_kv, cache_lf_t, layer, B, T):
    n_pages = page_table.shape[1]
    page = cache_kv.shape[2]
    P = n_pages * page
    tk = min(2048, P)
    grid_spec = pltpu.PrefetchScalarGridSpec(
        num_scalar_prefetch=1,
        grid=(B,),
        in_specs=[pl.BlockSpec((N_HEADS, T, HEAD_DIM), lambda b, pt: (0, b, 0)),
                  pl.BlockSpec((T, 128), lambda b, pt: (b, 0)),
                  pl.BlockSpec((T, 128), lambda b, pt: (b, 0)),
                  pl.BlockSpec(lstrict.shape, lambda b, pt: (0, 0)),
                  pl.BlockSpec(memory_space=pl.ANY),
                  pl.BlockSpec(memory_space=pl.ANY)],
        out_specs=pl.BlockSpec((T, N_HEADS * HEAD_DIM), lambda b, pt: (b, 0)),
        scratch_shapes=[pltpu.VMEM((2, P, 128), F32),
                        pltpu.VMEM((2, n_pages, N_HEADS, page), F32),
                        pltpu.VMEM((n_pages, N_HEADS, page), F32),
                        pltpu.SemaphoreType.DMA((2,)), pltpu.SemaphoreType.DMA((2,))])
    return pl.pallas_call(
        functools.partial(_fox_sample_body, layer=layer, n_pages=n_pages, page=page, tk=tk, T=T),
        grid_spec=grid_spec,
        out_shape=jax.ShapeDtypeStruct((B * T, N_HEADS * HEAD_DIM), F32),
        compiler_params=_cp("arbitrary"),
        name="forget_sample",
    )(page_table, fq, fox_rows, small, lstrict, cache_kv, cache_lf_t)


def _mla_sample_body(pt_ref, q_ref, new_ref, wukt_ref, wuv_ref, gkn_ref, gkr_ref, c_hbm, o_ref, buf, sem,
                     *, layer, n_pages, page, ppt, T):
    pairs = [(lambda pg: c_hbm.at[layer, pg], lambda s, p: buf.at[s, p], sem)]
    slot = _gather_pipeline(pt_ref, n_pages, page, pairs)
    H = N_HEADS
    R = H * T
    tk = ppt * page
    scale = MLA_QK ** -0.5

    ql, qr = [], []
    for h in range(H):
        qh = q_ref[h].astype(F32)
        ql.append(_mm(qh[:, 0:MLA_NOPE] * gkn_ref[...], wukt_ref[h]))
        qr.append(qh[:, MLA_NOPE:MLA_QK] * gkr_ref[...])
    ql = jnp.concatenate(ql, axis=0)
    qr = jnp.concatenate(qr, axis=0)
    qcat = jnp.concatenate([ql, qr], axis=1).astype(MXU_DTYPE)
    wukt = wukt_ref[...].reshape(H * MLA_NOPE, KV_LORA)

    def key_scale(knt, kr2, n):
        ssq = jnp.sum((knt * knt).reshape(H, MLA_NOPE, n), axis=1) + kr2
        return lax.rsqrt(ssq / MLA_QK + EPS) * scale

    def step(j, carry):
        m, l, acc = carry
        pt = jnp.concatenate([buf[slot, j * ppt + i] for i in range(ppt)], axis=1)
        krt = pt[KV_LORA:KV_LORA + MLA_ROPE]
        ptm = pt.astype(MXU_DTYPE)
        ckvt = ptm[0:KV_LORA]
        inv = key_scale(jnp.dot(wukt, ckvt, preferred_element_type=F32), jnp.sum(krt * krt, axis=0, keepdims=True), tk)
        s = jnp.dot(qcat, ptm, preferred_element_type=F32)
        s = (s.reshape(H, T, tk) * inv[:, None, :]).reshape(R, tk)
        return _softmax_update(s, None, m, l, acc, lambda w: _mm_nt(w, ckvt))

    init = (jnp.full((R, 1), NEG, F32), jnp.zeros((R, 1), F32), jnp.zeros((R, KV_LORA), F32))
    m, l, acc = lax.fori_loop(0, n_pages // ppt, step, init)

    rows = new_ref[...]
    ckv = rows[:, 0:KV_LORA]
    kr = rows[:, KV_LORA:KV_LORA + MLA_ROPE]
    inv = key_scale(_mm_nt(wukt, ckv), _mm3_nt_lhs_exact(jnp.ones((H, MLA_ROPE), MXU_DTYPE), kr * kr), T)
    s = (_mm_nt(qcat, rows).reshape(H, T, T) * inv[:, None, :]).reshape(R, T)
    mask = lax.broadcasted_iota(jnp.int32, (1, T), 1) <= _row_mod(R, T)
    m, l, acc = _softmax_update(s, mask, m, l, acc, ckv)
    lat = acc / jnp.maximum(l, 1e-30)
    for h in range(H):
        o_ref[:, h * HEAD_DIM:(h + 1) * HEAD_DIM] = _mm(lat[h * T:(h + 1) * T, :], wuv_ref[h])


def _mla_sample(page_table, mq, mla_rows, wukt, wuv, gkn, gkr, cache_t, layer, B, T):
    n_pages = page_table.shape[1]
    F, page = cache_t.shape[2], cache_t.shape[3]
    ppt = min(8, n_pages)
    assert n_pages % ppt == 0
    full = lambda a: pl.BlockSpec(a.shape, lambda b, pt: (0,) * a.ndim)
    grid_spec = pltpu.PrefetchScalarGridSpec(
        num_scalar_prefetch=1,
        grid=(B,),
        in_specs=[pl.BlockSpec((N_HEADS, T, MLA_QK), lambda b, pt: (0, b, 0)),
                  pl.BlockSpec((T, F), lambda b, pt: (b, 0)),
                  full(wukt), full(wuv), full(gkn), full(gkr),
                  pl.BlockSpec(memory_space=pl.ANY)],
        out_specs=pl.BlockSpec((T, N_HEADS * HEAD_DIM), lambda b, pt: (b, 0)),
        scratch_shapes=[pltpu.VMEM((2, n_pages, F, page), F32), pltpu.SemaphoreType.DMA((2,))])
    return pl.pallas_call(
        functools.partial(_mla_sample_body, layer=layer, n_pages=n_pages, page=page, ppt=ppt, T=T),
        grid_spec=grid_spec,
        out_shape=jax.ShapeDtypeStruct((B * T, N_HEADS * HEAD_DIM), F32),
        compiler_params=_cp("arbitrary"),
        name="latent_sample",
    )(page_table, mq, mla_rows, wukt, wuv, gkn, gkr, cache_t)


def _nsa_sample_body(pt_ref, nq_ref, nqr_ref, small_ref, newrows_ref, newwin_ref, pastwin_ref, pe_ref, w_ref, g_ref,
                     selw_ref, et_ref, c_hbm, o_ref, winout_ref, bufc, bufs, sem_c, sem_s, *, layer, n_pages, page, tk,
                     T, k_top):
    pairs = [(lambda pg: c_hbm.at[layer, pg, :, pl.ds(0, 128)], lambda s, p: bufc.at[s, pl.ds(p * page, page)], sem_c),
             (lambda pg: c_hbm.at[layer, pg, :, pl.ds(128, 128)], lambda s, p: bufs.at[s, pl.ds(p * page, page)], sem_s)]
    slot = _gather_pipeline(pt_ref, n_pages, page, pairs)
    R = N_HEADS * T
    P = n_pages * page
    nch = P // STRIDE_CMP
    nsp = selw_ref.shape[1]
    wb = pastwin_ref.shape[0]
    qidx = _row_mod(R, T)
    qpos = P + qidx
    qn = nq_ref[...].reshape(R, HEAD_DIM)
    qr = nqr_ref[...].reshape(R, HEAD_DIM)
    new_mask = lax.broadcasted_iota(jnp.int32, (1, T), 1) <= qidx
    init = (jnp.full((R, 1), NEG, F32), jnp.zeros((R, 1), F32), jnp.zeros((R, HEAD_DIM), F32))

    kcv = _compress(lambda l: bufc[slot, pl.ds(l, nch, stride=STRIDE_CMP), :], nch, pe_ref, w_ref, g_ref)
    cmp_end = lax.broadcasted_iota(jnp.int32, (1, nch), 1) * STRIDE_CMP + (L_CMP - 1)
    mask_c = cmp_end <= qpos
    s = jnp.where(mask_c, _mm_nt(qn, kcv[:, 0:HEAD_DIM]), NEG)
    p = jnp.where(mask_c, jnp.exp(s - jnp.max(s, axis=-1, keepdims=True)), 0.0)
    p = p / jnp.maximum(jnp.sum(p, axis=-1, keepdims=True), 1e-30)
    o_c = _mm(p, kcv[:, HEAD_DIM:])
    imp = _mm3(jnp.sum(p.reshape(N_HEADS, T, nch), axis=0), selw_ref[...])

    ids = lax.broadcasted_iota(jnp.int32, (1, nsp), 1)
    qp = P + lax.broadcasted_iota(jnp.int32, (T, 1), 0)
    cur = qp // L_SEL
    forced = (ids == 0) | (ids == cur) | (ids == cur - 1)
    score = jnp.where(forced, jnp.inf, jnp.where(ids * L_SEL <= qp, imp, -jnp.inf))
    selm = _topk_mask(score, ids, k_top).astype(MXU_DTYPE)
    selm = jnp.concatenate([selm] * N_HEADS, axis=0)

    def sel_step(j, carry):
        m, l, acc = carry
        k0 = pl.multiple_of(j * tk, tk)
        blk = _mm_nt(selm, et_ref[pl.ds(k0, tk), :]) > 0.5
        return _softmax_update(_mm_nt(qr, bufs[slot, pl.ds(k0, tk), 0:HEAD_DIM]), blk, m, l, acc,
                               bufs[slot, pl.ds(k0, tk), HEAD_DIM:])

    m, l, acc = lax.fori_loop(0, P // tk, sel_step, init)
    blk = (_mm_nt(selm, et_ref[pl.ds(P, T), :]) > 0.5) & new_mask
    m, l, acc = _softmax_update(_mm_nt(qr, newrows_ref[:, 128:192]), blk, m, l, acc, newrows_ref[:, 192:256])
    o_s = acc / jnp.maximum(l, 1e-30)

    mask_w = lax.broadcasted_iota(jnp.int32, (1, wb), 1) > qidx + (wb - WINDOW)
    m, l, acc = _softmax_update(_mm_nt(qr, pastwin_ref[:, 0:HEAD_DIM]), mask_w, *init, pastwin_ref[:, HEAD_DIM:])
    m, l, acc = _softmax_update(_mm_nt(qr, newwin_ref[:, 0:HEAD_DIM]), new_mask, m, l, acc, newwin_ref[:, HEAD_DIM:])
    o_w = acc / jnp.maximum(l, 1e-30)

    gates = [jnp.concatenate([small_ref[:, S_GATE + 3 * h + j:S_GATE + 3 * h + j + 1] for h in range(N_HEADS)], axis=0)
             for j in range(3)]
    _heads_to_lanes(o_ref, gates[0] * o_c + gates[1] * o_s + gates[2] * o_w, T, HEAD_DIM)

    keep = winout_ref.shape[0] - T
    winout_ref[0:keep, :] = pastwin_ref[wb - keep:wb, :]
    winout_ref[keep:keep + T, :] = newwin_ref[...]


def _nsa_sample(page_table, nq, nqr, small, nsa_rows, win_rows, state_win, pe, wblk, gk, selw, et, cache, layer, B, T):
    n_pages = page_table.shape[1]
    page = cache.shape[2]
    P = n_pages * page
    wb = state_win.shape[2]
    assert T < STRIDE_CMP and P % STRIDE_CMP == 0 and P + T >= WINDOW and wb >= WINDOW - T
    tk = min(2048, P)
    k_top = min(N_SELECT, -(-(P + T) // L_SEL))
    full = lambda a: pl.BlockSpec(a.shape, lambda b, pt: (0,) * a.ndim)
    head = pl.BlockSpec((N_HEADS, T, HEAD_DIM), lambda b, pt: (0, b, 0))
    grid_spec = pltpu.PrefetchScalarGridSpec(
        num_scalar_prefetch=1,
        grid=(B,),
        in_specs=[head, head,
                  pl.BlockSpec((T, 128), lambda b, pt: (b, 0)),
                  pl.BlockSpec((T, 256), lambda b, pt: (b, 0)),
                  pl.BlockSpec((T, 128), lambda b, pt: (b, 0)),
                  pl.BlockSpec((None, None, wb, 128), lambda b, pt: (layer, b, 0, 0)),
                  full(pe), full(wblk), full(gk), full(selw), full(et),
                  pl.BlockSpec(memory_space=pl.ANY)],
        out_specs=[pl.BlockSpec((T, N_HEADS * HEAD_DIM), lambda b, pt: (b, 0)),
                   pl.BlockSpec((None, WINDOW, 128), lambda b, pt: (b, 0, 0))],
        scratch_shapes=[pltpu.VMEM((2, P, 128), F32), pltpu.VMEM((2, P, 128), F32),
                        pltpu.SemaphoreType.DMA((2,)), pltpu.SemaphoreType.DMA((2,))])
    return pl.pallas_call(
        functools.partial(_nsa_sample_body, layer=layer, n_pages=n_pages, page=page, tk=tk, T=T, k_top=k_top),
        grid_spec=grid_spec,
        out_shape=[jax.ShapeDtypeStruct((B * T, N_HEADS * HEAD_DIM), F32),
                   jax.ShapeDtypeStruct((B, WINDOW, 128), F32)],
        compiler_params=_cp("arbitrary"),
        name="nsa_sample",
    )(page_table, nq, nqr, small, nsa_rows, win_rows, state_win, pe, wblk, gk, selw, et, cache)


def _block_ones(n, g):
    i = np.arange(n)
    return jnp.asarray(i[:, None] // g == i[None, :] // g, MXU_DTYPE)


def _sel_tables(S, nch):
    n_cmp = (S - L_CMP) // STRIDE_CMP + 1
    n_sel = -(-S // L_SEL)
    nsp = -(-n_sel // 128) * 128
    cs = np.arange(nch)[:, None] * STRIDE_CMP
    ss = np.arange(nsp)[None, :] * L_SEL
    w = np.maximum(np.minimum(cs + L_CMP, ss + L_SEL) - np.maximum(cs, ss), 0) // STRIDE_CMP
    w = np.where((np.arange(nch)[:, None] < n_cmp) & (np.arange(nsp)[None, :] < n_sel), w, 0)
    s_pad = -(-S // 16) * 16
    et = np.arange(s_pad)[:, None] // L_SEL == np.arange(nsp)[None, :]
    return jnp.asarray(w, MXU_DTYPE), jnp.asarray(et, MXU_DTYPE), n_sel


def _row_tile(M, pref):
    return pref if M % pref == 0 else M


def kernel(x_prompt, x_sample, cache_nsa, cache_fox_kv, cache_fox_logf, cache_mla, state_nsa_win, cache_mem, page_table, mem_prompt, norm_mix, w_in, b_fox, nsa_q_norm, nsa_k_norm, nsa_pe, nsa_w_cmp, fox_q_norm, fox_k_norm, mla_q_norm, mla_kv_norm, mla_w_uk, mla_w_uv, mla_k_norm, w_branch, w_o, norm_mem, mem_in_norm, w_q_mem, w_kv_mem, mem_q_norm, mem_k_norm, w_o_mem, norm_mlp, w_up, w_down):
    B, T, D = x_prompt.shape
    Bs, Ts, _ = x_sample.shape
    depth = w_in.shape[0]
    n_pages = page_table.shape[1]
    page = cache_nsa.shape[2]
    P = n_pages * page
    n_mem = mem_prompt.shape[1]
    bf = lambda a: a.astype(MXU_DTYPE)
    ones = lambda n: jnp.ones((depth, n), F32)

    splits = np.cumsum([0, 512, 384, 24, 512, 128, 8, 768, KV_LORA, MLA_ROPE, N_BRANCH * D])
    col = lambda i: w_in[:, :, splits[i]:splits[i + 1]]
    mla_q = col(6).reshape(depth, D, N_HEADS, MLA_QK)
    small_cols = jnp.concatenate([col(2), col(5), col(8), jnp.zeros((depth, D, 128 - 24 - 8 - MLA_ROPE), F32)], axis=2)
    w_rest = bf(jnp.concatenate([col(0), col(1), col(3), col(4),
                                 mla_q[..., :MLA_NOPE].reshape(depth, D, N_HEADS * MLA_NOPE),
                                 mla_q[..., MLA_NOPE:].reshape(depth, D, N_HEADS * MLA_ROPE),
                                 col(7), small_cols], axis=2))
    w_gate = bf(col(9))
    gqn = jnp.tile(nsa_q_norm, (1, N_HEADS))
    gkv = jnp.concatenate([ones(128), nsa_k_norm[:, 1], ones(64), nsa_k_norm[:, 2], ones(64)], axis=1)
    mkv = jnp.asarray(np.repeat([0, 0, 1, 0, 1, 0], HEAD_DIM)[None, :], F32)
    gqf = jnp.tile(fox_q_norm, (1, N_HEADS))
    gfk = jnp.concatenate([fox_k_norm, ones(64)], axis=1)
    gmn = jnp.tile(mla_q_norm[:, :MLA_NOPE], (1, N_HEADS))
    gmr = jnp.tile(mla_q_norm[:, MLA_NOPE:], (1, N_HEADS))
    bfox = jnp.concatenate([jnp.zeros((depth, S_LOGF), F32), b_fox, jnp.zeros((depth, 128 - S_LOGF - N_HEADS), F32)], axis=1)
    g64 = _block_ones(512, HEAD_DIM)
    g32 = _block_ones(256, MLA_ROPE)
    gx = jnp.asarray(np.arange(256)[:, None] // MLA_ROPE == np.arange(512)[None, :] // HEAD_DIM, MXU_DTYPE)
    gxt = gx.T
    pe = jnp.concatenate([nsa_pe[:, 0], nsa_pe[:, 1]], axis=2)
    zero = jnp.zeros_like(nsa_w_cmp[:, 0])
    wblk = bf(jnp.concatenate([jnp.concatenate([nsa_w_cmp[:, 0], zero], axis=3),
                               jnp.concatenate([zero, nsa_w_cmp[:, 1]], axis=3)], axis=2))
    gkc = jnp.concatenate([nsa_k_norm[:, 0], ones(64)], axis=1)
    wuk = bf(mla_w_uk.reshape(depth, KV_LORA, N_HEADS * MLA_NOPE))
    wuv = bf(mla_w_uv.reshape(depth, KV_LORA, N_HEADS * HEAD_DIM))
    wukt = bf(jnp.transpose(mla_w_uk, (0, 2, 3, 1)))
    wuvh = bf(jnp.transpose(mla_w_uv, (0, 2, 1, 3)))
    gkn = jnp.tile(mla_k_norm[:, :MLA_NOPE], (1, N_HEADS))
    wb_, wo_ = bf(w_branch), bf(w_o)
    wqm, wkvm, wom = bf(w_q_mem), bf(w_kv_mem), bf(w_o_mem)
    wu, wd = bf(w_up), bf(w_down)

    tm_post_p = _row_tile(T, 256)
    tm_post_s = _row_tile(Bs * Ts, 256)
    tab_p = _rope_tables(jnp.arange(T).astype(F32))
    tab_s = jnp.tile(_rope_tables((P + jnp.arange(Ts)).astype(F32)), (tm_post_s // Ts, 1))
    selw_p, et_p, _ = _sel_tables(T, T // STRIDE_CMP)
    selw_s, et_s, _ = _sel_tables(P + Ts, P // STRIDE_CMP)

    xp = x_prompt.reshape(B * T, D)
    xs = x_sample.reshape(Bs * Ts, D)
    tm_p = _row_tile(B * T, 512)
    tm_s = _row_tile(Bs * Ts, 512)
    tq = _row_tile(T, 128)
    tkf = min(512, T)
    cache_mla_t = jnp.swapaxes(cache_mla, 2, 3)
    cache_lf_t = jnp.swapaxes(cache_fox_logf, 2, 3)
    pr = np.arange(n_pages * N_HEADS)
    lstrict = jnp.asarray((pr[None, :] % N_HEADS == pr[:, None] % N_HEADS) & (pr[None, :] // N_HEADS < pr[:, None] // N_HEADS),
                          MXU_DTYPE)
    outs = [[] for _ in range(11)]
    r2 = lambda a: a[None, :]

    for l in range(depth):
        post_consts = [r2(gqn[l]), r2(gkv[l]), mkv, r2(gqf[l]), r2(gfk[l]), r2(gmn[l]), r2(gmr[l]),
                       r2(mla_kv_norm[l]), r2(bfox[l]), g64, gx, gxt, g32]
        cmp_consts = (pe[l], wblk[l], r2(gkc[l]))

        def mixers_in(x, tm, tab, Tg, tm_post):
            g = _norm_matmul(x, r2(norm_mix[l]), w_gate[l], tm)
            p = _norm_matmul(x, r2(norm_mix[l]), w_rest[l], tm)
            return (g,) + tuple(_post(p, tab, Tg, post_consts, tm_post))

        def tail(x, g, o_nsa, o_fox, o_mla, mem_kv, Bg, Tg, tm):
            x = _merge(x, o_nsa, o_fox, o_mla, g, wb_[l], wo_[l], tm)
            x = _mem_attn(x, mem_kv, r2(norm_mem[l]), wqm[l], r2(mem_q_norm[l]), wom[l], Bg, Tg, _row_tile(Tg, 256))
            return _mlp(x, r2(norm_mlp[l]), wu[l], wd[l], tm, min(1024, wu.shape[2]))

        mem_p = _mem_kv_rows(mem_prompt.reshape(B * n_mem, D), r2(mem_in_norm[l]), wkvm[l], r2(mem_k_norm[l]),
                             _row_tile(B * n_mem, 512)).reshape(B, n_mem, 2 * MEM_W)
        g, nq, nqr, fq, mq, nsa_rows, win_rows, fox_rows, mla_rows, small = mixers_in(xp, tm_p, tab_p, T, tm_post_p)
        kcv = _nsa_compress(nsa_rows, B, T, *cmp_consts)
        o_nsa = _nsa_prompt(nq, nqr, small, nsa_rows, win_rows, kcv, selw_p, et_p, B, T, tq)
        o_fox = _flash_prompt(fq, fox_rows, None, _fox_cumsum(small, B, T, tkf), B, T, tq, tkf, scale=1.0)
        kk, vv = _mla_kv(mla_rows, B, T, wuk[l], wuv[l], r2(gkn[l]), r2(mla_k_norm[l, MLA_NOPE:]), g64, tkf)
        o_mla = _flash_prompt(mq, kk, vv, None, B, T, tq, tkf, scale=MLA_QK ** -0.5)
        xp = tail(xp, g, o_nsa, o_fox, o_mla, mem_p, B, T, tm_p)
        wkeep = min(WINDOW, T)
        for i, a in enumerate((nsa_rows.reshape(B, T, -1), fox_rows.reshape(B, T, -1),
                               small[:, S_LOGF:S_LOGF + N_HEADS].reshape(B, T, -1), mla_rows.reshape(B, T, -1),
                               win_rows.reshape(B, T, -1)[:, T - wkeep:], mem_p)):
            outs[i].append(a)

        g, nq, nqr, fq, mq, nsa_rows, win_rows, fox_rows, mla_rows, small = mixers_in(xs, tm_s, tab_s, Ts, tm_post_s)
        o_nsa, new_win = _nsa_sample(page_table, nq, nqr, small, nsa_rows, win_rows, state_nsa_win, *cmp_consts,
                                     selw_s, et_s, cache_nsa, l, Bs, Ts)
        o_fox = _fox_sample(page_table, fq, fox_rows, small, lstrict, cache_fox_kv, cache_lf_t, l, Bs, Ts)
        o_mla = _mla_sample(page_table, mq, mla_rows, wukt[l], wuvh[l], r2(mla_k_norm[l, :MLA_NOPE]),
                            r2(mla_k_norm[l, MLA_NOPE:]), cache_mla_t, l, Bs, Ts)
        xs = tail(xs, g, o_nsa, o_fox, o_mla, cache_mem[l], Bs, Ts, tm_s)
        for i, a in enumerate((nsa_rows.reshape(Bs, Ts, -1), fox_rows.reshape(Bs, Ts, -1),
                               small[:, S_LOGF:S_LOGF + N_HEADS].reshape(Bs, Ts, -1), mla_rows.reshape(Bs, Ts, -1),
                               new_win)):
            outs[6 + i].append(a)

    return (xp.reshape(B, T, D), xs.reshape(Bs, Ts, D)) + tuple(jnp.stack(o) for o in outs)
```

```python
import functools
import math

import jax
import jax.numpy as jnp
import numpy as np
from jax import lax
from jax.experimental import pallas as pl
from jax.experimental.pallas import tpu as pltpu

F32 = jnp.float32
MXU_DTYPE = jnp.bfloat16

HEAD_DIM = 64
N_HEADS = 8
MLA_NOPE = 64
MLA_ROPE = 32
MLA_QK = MLA_NOPE + MLA_ROPE
KV_LORA = 256
N_BRANCH = 3
L_CMP = 32
STRIDE_CMP = 16
L_SEL = 64
N_SELECT = 16
WINDOW = 512
N_MEM_HEADS = 4
MEM_HD = 128
MEM_W = N_MEM_HEADS * MEM_HD
ROPE_THETA = 10000.0
EPS = 1e-6
NEG = -1e30
HEAD_GROUPS = 2

P_QN, P_KVN, P_QF, P_KVF, P_QMN, P_QMR, P_CKV, P_SMALL, P_W = 0, 512, 896, 1408, 1536, 2048, 2304, 2560, 2688
S_GATE, S_LOGF, S_KR = 0, 24, 32

VMEM_LIMIT = 56 * 1024 * 1024


def _cp(*sem):
    return pltpu.CompilerParams(dimension_semantics=sem, vmem_limit_bytes=VMEM_LIMIT)


def _mm(a, b):
    return jnp.dot(a.astype(MXU_DTYPE), b.astype(MXU_DTYPE), preferred_element_type=F32)


def _mm_nt(a, b):
    return lax.dot_general(a.astype(MXU_DTYPE), b.astype(MXU_DTYPE), (((1,), (1,)), ((), ())),
                           preferred_element_type=F32)


def _split3(x):
    hi = x.astype(MXU_DTYPE)
    r = x - hi.astype(F32)
    mid = r.astype(MXU_DTYPE)
    lo = (r - mid.astype(F32)).astype(MXU_DTYPE)
    return hi, mid, lo


def _mm3(a, b_exact):
    b = b_exact.astype(MXU_DTYPE)
    return sum(jnp.dot(p, b, preferred_element_type=F32) for p in _split3(a))


def _mm3_nt_lhs_exact(a_exact, b):
    a = a_exact.astype(MXU_DTYPE)
    dn = (((1,), (1,)), ((), ()))
    return sum(lax.dot_general(a, p, dn, preferred_element_type=F32) for p in _split3(b))


def _rms(x, g):
    return x * lax.rsqrt(jnp.mean(x * x, axis=-1, keepdims=True) + EPS) * g


def _mm3_lhs_exact(a_exact, b):
    a = a_exact.astype(MXU_DTYPE)
    return sum(jnp.dot(a, p, preferred_element_type=F32) for p in _split3(b))


def _softmax_update(s, mask, m_old, l_old, acc_old, v):
    if mask is not None:
        s = jnp.where(mask, s, NEG)
    m_new = jnp.maximum(m_old, jnp.max(s, axis=-1, keepdims=True))
    alpha = jnp.exp(m_old - m_new)
    p = jnp.exp(s - m_new)
    if mask is not None:
        p = jnp.where(mask, p, 0.0)
    l_new = alpha * l_old + jnp.sum(p, axis=-1, keepdims=True)
    acc_new = alpha * acc_old + (v(p) if callable(v) else _mm(p, v))
    return m_new, l_new, acc_new


def _masked_softmax(s, mask):
    s = jnp.where(mask, s, NEG)
    p = jnp.where(mask, jnp.exp(s - jnp.max(s, axis=-1, keepdims=True)), 0.0)
    return p / jnp.maximum(jnp.sum(p, axis=-1, keepdims=True), 1e-30)


def _stack_rows(x3):
    return x3.reshape(x3.shape[0] * x3.shape[1], x3.shape[2])


def _topk_mask(score, n_valid, k_top):
    rows, n = score.shape
    rpad = -(-rows // 128) * 128
    if rpad != rows:
        score = jnp.concatenate([score, jnp.full((rpad - rows, n), -jnp.inf, F32)], axis=0)
    nv = -(-n_valid // 8) * 8
    st = score.T[:nv]
    ids = lax.broadcasted_iota(jnp.int32, st.shape, 0)
    sel = jnp.zeros(st.shape, F32)
    for _ in range(k_top):
        mx = jnp.max(st, axis=0, keepdims=True)
        idx = jnp.min(jnp.where(st == mx, ids, 1 << 30), axis=0, keepdims=True)
        pick = ids == idx
        sel = jnp.where(pick & (mx > -jnp.inf), 1.0, sel)
        st = jnp.where(pick, -jnp.inf, st)
    if nv != n:
        sel = jnp.concatenate([sel, jnp.zeros((n - nv, rpad), F32)], axis=0)
    return sel.T[:rows]


def _rope_tab_body(pos_ref, inv_ref, o_ref):
    ang = pos_ref[...] * inv_ref[...]
    c = jnp.cos(ang)
    s = jnp.sin(ang)
    lane = lax.broadcasted_iota(jnp.int32, (ang.shape[0], 128), 1)
    first32 = (lane % 64) < 32
    first16 = (lane % 32) < 16
    c32, s32 = c[:, :128], s[:, :128]
    c16, s16 = c[:, 128:], s[:, 128:]
    o_ref[:, 0:128] = c32
    o_ref[:, 128:256] = jnp.where(first32, -s32, 0.0)
    o_ref[:, 256:384] = jnp.where(first32, 0.0, s32)
    o_ref[:, 384:512] = c16
    o_ref[:, 512:640] = jnp.where(first16, -s16, 0.0)
    o_ref[:, 640:768] = jnp.where(first16, 0.0, s16)


def _rope_tables(pos):
    T = pos.shape[0]
    inv32 = ROPE_THETA ** (-jnp.arange(32, dtype=F32) / 32)
    inv16 = ROPE_THETA ** (-jnp.arange(16, dtype=F32) / 16)
    inv = jnp.concatenate([jnp.tile(inv32, 4), jnp.tile(inv16, 8)])[None, :]
    return pl.pallas_call(
        _rope_tab_body,
        name="rope_tables",
        out_shape=jax.ShapeDtypeStruct((T, 768), F32),
    )(pos[:, None], inv)


def _norm_mm_body(x_ref, g_ref, w_ref, o_ref):
    o_ref[...] = _mm(_rms(x_ref[...], g_ref[...]), w_ref[...])


def _norm_matmul(x, g, w, tm):
    M, K = x.shape
    N = w.shape[1]
    return pl.pallas_call(
        _norm_mm_body,
        name="norm_matmul",
        grid=(M // tm,),
        in_specs=[pl.BlockSpec((tm, K), lambda i: (i, 0)),
                  pl.BlockSpec((1, K), lambda i: (0, 0)),
                  pl.BlockSpec((K, N), lambda i: (0, 0))],
        out_specs=pl.BlockSpec((tm, N), lambda i: (i, 0)),
        out_shape=jax.ShapeDtypeStruct((M, N), F32),
        compiler_params=_cp("parallel"),
    )(x, g, w)


def _mem_kv_body(x_ref, g_ref, w_ref, gk_ref, o_ref):
    kv = _mm(_rms(x_ref[...], g_ref[...]), w_ref[...])
    for h in range(N_MEM_HEADS):
        sl = slice(h * MEM_HD, (h + 1) * MEM_HD)
        o_ref[:, sl] = _rms(kv[:, sl], gk_ref[...])
    o_ref[:, MEM_W:] = kv[:, MEM_W:]


def _mem_kv_rows(mem2d, g_in, w_kv, g_k, tm):
    M, K = mem2d.shape
    N = w_kv.shape[1]
    return pl.pallas_call(
        _mem_kv_body,
        name="mem_kv_rows",
        grid=(M // tm,),
        in_specs=[pl.BlockSpec((tm, K), lambda i: (i, 0)),
                  pl.BlockSpec((1, K), lambda i: (0, 0)),
                  pl.BlockSpec((K, N), lambda i: (0, 0)),
                  pl.BlockSpec((1, MEM_HD), lambda i: (0, 0))],
        out_specs=pl.BlockSpec((tm, N), lambda i: (i, 0)),
        out_shape=jax.ShapeDtypeStruct((M, N), F32),
        compiler_params=_cp("parallel"),
    )(mem2d, g_in, w_kv, g_k)


def _post_body(p_ref, tab_ref, gqn_ref, gkv_ref, mkv_ref, gqf_ref, gfk_ref, gmn_ref, gmr_ref, gckv_ref,
               bfox_ref, g64_ref, gx_ref, gxt_ref, g32_ref,
               nq_ref, nqr_ref, fq_ref, mq_ref, nsa_ref, win_ref, fox_ref, mla_ref, small_ref):
    tab = tab_ref[...]
    c32, sp32, sm32, c16, sp16, sm16 = [tab[:, i * 128:(i + 1) * 128] for i in range(6)]

    def tile(t, n):
        return t if n == 1 else jnp.concatenate([t] * n, axis=1)

    def rope(x, c, sp, sm, s):
        w = x.shape[1]
        n = w // 128
        return x * tile(c, n) + pltpu.roll(x, w - s, 1) * tile(sp, n) + pltpu.roll(x, s, 1) * tile(sm, n)

    g64 = g64_ref[...]
    scale = HEAD_DIM ** -0.5

    x = p_ref[:, P_QN:P_QN + 512]
    y = x * lax.rsqrt(_mm3(x * x, g64) / HEAD_DIM + EPS) * gqn_ref[...]
    yr = rope(y, c32, sp32, sm32, 32)
    xf = p_ref[:, P_QF:P_QF + 512]
    yf = xf * lax.rsqrt(_mm3(xf * xf, g64) / HEAD_DIM + EPS) * gqf_ref[...]
    for h in range(N_HEADS):
        sl = slice(h * HEAD_DIM, (h + 1) * HEAD_DIM)
        nq_ref[h] = (y[:, sl] * scale).astype(nq_ref.dtype)
        nqr_ref[h] = (yr[:, sl] * scale).astype(nqr_ref.dtype)
        fq_ref[h] = (yf[:, sl] * scale).astype(fq_ref.dtype)

    x = p_ref[:, P_KVN:P_KVN + 384]
    normed = mkv_ref[...] > 0.5
    yn = x * lax.rsqrt(_mm3(x * x, g64[:384, :384]) / HEAD_DIM + EPS) * gkv_ref[...]
    y = jnp.where(normed, rope(yn, c32, sp32, sm32, 32), x)
    nsa_ref[...] = y[:, :256]
    win_ref[...] = y[:, 256:]

    x = p_ref[:, P_KVF:P_KVF + 128]
    lane = lax.broadcasted_iota(jnp.int32, x.shape, 1)
    yn = x * lax.rsqrt(_mm3(x * x, g64[:128, :128]) / HEAD_DIM + EPS) * gfk_ref[...]
    fox_ref[...] = jnp.where(lane < HEAD_DIM, yn, x)

    xn = p_ref[:, P_QMN:P_QMN + 512]
    xr = rope(p_ref[:, P_QMR:P_QMR + 256], c16, sp16, sm16, 16)
    xn2, xr2 = xn * xn, xr * xr
    ms_n = (_mm3(xn2, g64) + _mm3(xr2, gx_ref[...])) / MLA_QK
    ms_r = (_mm3(xn2, gxt_ref[...]) + _mm3(xr2, g32_ref[...])) / MLA_QK
    yn = xn * lax.rsqrt(ms_n + EPS) * gmn_ref[...]
    yr = xr * lax.rsqrt(ms_r + EPS) * gmr_ref[...]
    for h in range(N_HEADS):
        mq_ref[h, :, 0:MLA_NOPE] = yn[:, h * MLA_NOPE:(h + 1) * MLA_NOPE].astype(mq_ref.dtype)
        mq_ref[h, :, MLA_NOPE:MLA_QK] = yr[:, h * MLA_ROPE:(h + 1) * MLA_ROPE].astype(mq_ref.dtype)

    x = p_ref[:, P_SMALL:P_SMALL + 128]
    sig = jax.nn.sigmoid(x)
    lf = jax.nn.log_sigmoid(x + bfox_ref[...])
    rr = rope(x, c16, sp16, sm16, 16)
    small = jnp.where(lane < S_LOGF, sig, jnp.where(lane < S_KR, lf, jnp.where(lane < S_KR + MLA_ROPE, rr, 0.0)))
    small_ref[...] = small

    x = p_ref[:, P_CKV:P_CKV + KV_LORA]
    mla_ref[:, 0:KV_LORA] = _rms(x, gckv_ref[...])
    mla_ref[:, KV_LORA:KV_LORA + MLA_ROPE] = small[:, S_KR:S_KR + MLA_ROPE]


def _post(p, tab, T, consts, tm):
    M = p.shape[0]
    if T >= tm:
        nt = T // tm
        tab_map = lambda i: (i % nt, 0)
    else:
        tab_map = lambda i: (0, 0)
    row = lambda w: pl.BlockSpec((tm, w), lambda i: (i, 0))
    head = lambda d: pl.BlockSpec((N_HEADS, tm, d), lambda i: (0, i, 0))
    full = lambda a: pl.BlockSpec(a.shape, lambda i: (0,) * a.ndim)
    qd = MXU_DTYPE
    out_shape = [jax.ShapeDtypeStruct((N_HEADS, M, HEAD_DIM), qd)] * 3 + [
        jax.ShapeDtypeStruct((N_HEADS, M, MLA_QK), qd),
        jax.ShapeDtypeStruct((M, 256), F32), jax.ShapeDtypeStruct((M, 128), F32),
        jax.ShapeDtypeStruct((M, 128), F32), jax.ShapeDtypeStruct((M, KV_LORA + MLA_ROPE), F32),
        jax.ShapeDtypeStruct((M, 128), F32)]
    out_specs = [head(HEAD_DIM)] * 3 + [head(MLA_QK), row(256), row(128), row(128), row(KV_LORA + MLA_ROPE), row(128)]
    return pl.pallas_call(
        _post_body,
        name="post",
        grid=(M // tm,),
        in_specs=[row(P_W), pl.BlockSpec((tm, 768), tab_map)] + [full(c) for c in consts],
        out_specs=out_specs,
        out_shape=out_shape,
        compiler_params=_cp("parallel"),
    )(p, tab, *consts)


def _compress(load_xl, nch, pe_ref, w_ref, g_ref):
    acc_a = jnp.zeros((nch, 128), F32)
    acc_b = jnp.zeros((nch, 128), F32)
    for l in range(STRIDE_CMP):
        xl = load_xl(l)
        acc_a = acc_a + _mm(xl + pe_ref[l:l + 1, :], w_ref[l])
        acc_b = acc_b + _mm(xl + pe_ref[STRIDE_CMP + l:STRIDE_CMP + l + 1, :], w_ref[STRIDE_CMP + l])
    kcv = acc_a + pltpu.roll(acc_b, nch - 1, 0)
    lane = lax.broadcasted_iota(jnp.int32, kcv.shape, 1)
    is_k = lane < HEAD_DIM
    ms = jnp.sum(jnp.where(is_k, kcv * kcv, 0.0), axis=-1, keepdims=True) / HEAD_DIM
    return jnp.where(is_k, kcv * lax.rsqrt(ms + EPS) * g_ref[...], kcv)


def _cmp_body(rows_ref, pe_ref, w_ref, g_ref, o_ref, *, nch):
    o_ref[...] = _compress(lambda l: rows_ref[pl.ds(l, nch, stride=STRIDE_CMP), :], nch, pe_ref, w_ref, g_ref)


def _nsa_compress(nsa_rows, B, S, pe, wblk, gk):
    nch = S // STRIDE_CMP
    return pl.pallas_call(
        functools.partial(_cmp_body, nch=nch),
        name="nsa_compress",
        grid=(B,),
        in_specs=[pl.BlockSpec((S, 128), lambda b: (b, 0)),
                  pl.BlockSpec(pe.shape, lambda b: (0, 0)),
                  pl.BlockSpec(wblk.shape, lambda b: (0, 0, 0)),
                  pl.BlockSpec((1, 128), lambda b: (0, 0))],
        out_specs=pl.BlockSpec((None, nch, 128), lambda b: (b, 0, 0)),
        out_shape=jax.ShapeDtypeStruct((B, nch, 128), F32),
        compiler_params=_cp("parallel"),
    )(nsa_rows, pe, wblk, gk)


def _nsa_prompt_body(nq_ref, nqr_ref, small_ref, rows_ref, win_ref, kcv_ref, selw_ref, et_ref, o_ref,
                     m_ref, l_ref, acc_ref, *, tq, tk, wlen, k_top):
    H = N_HEADS
    G = HEAD_GROUPS
    Hg = H // G
    groups = [slice(g * Hg, (g + 1) * Hg) for g in range(G)]
    T = rows_ref.shape[0]
    qi = pl.program_id(1)
    q0 = qi * tq
    qpos = q0 + lax.broadcasted_iota(jnp.int32, (tq, 1), 0)
    nch = kcv_ref.shape[0]
    nsp = selw_ref.shape[1]

    def gate(j, hs):
        return jnp.concatenate([small_ref[:, S_GATE + 3 * h + j:S_GATE + 3 * h + j + 1]
                                for h in range(hs.start, hs.stop)], axis=0).reshape(Hg, tq, 1)

    kc = kcv_ref[:, 0:HEAD_DIM]
    vc = kcv_ref[:, HEAD_DIM:]
    cmp_end = lax.broadcasted_iota(jnp.int32, (1, nch), 1) * STRIDE_CMP + (L_CMP - 1)
    mask_c = (cmp_end <= qpos)[None]
    out = []
    psum = jnp.zeros((tq, nch), F32)
    for hs in groups:
        p = _masked_softmax(_mm_nt(_stack_rows(nq_ref[hs]), kc).reshape(Hg, tq, nch), mask_c)
        out.append(gate(0, hs) * _mm(_stack_rows(p), vc).reshape(Hg, tq, HEAD_DIM))
        psum = psum + jnp.sum(p, axis=0)
    imp = _mm3(psum, selw_ref[...])

    ids = lax.broadcasted_iota(jnp.int32, (1, nsp), 1)
    cur = qpos // L_SEL
    forced = (ids == 0) | (ids == cur) | (ids == cur - 1)
    avail = ids * L_SEL <= qpos
    score = jnp.where(forced, jnp.inf, jnp.where(avail, imp, -jnp.inf))
    selm = _topk_mask(score, -(-T // L_SEL), k_top).astype(MXU_DTYPE)

    m_ref[...] = jnp.full(m_ref.shape, NEG, F32)
    l_ref[...] = jnp.zeros(l_ref.shape, F32)
    acc_ref[...] = jnp.zeros(acc_ref.shape, F32)

    def sel_step(kt, carry):
        k0 = pl.multiple_of(kt * tk, tk)
        blk = _mm_nt(selm, et_ref[pl.ds(k0, tk), :]) > 0.5
        kpos = k0 + lax.broadcasted_iota(jnp.int32, (1, tk), 1)
        mask = (blk & (kpos <= qpos))[None]
        kk = rows_ref[pl.ds(k0, tk), 128:192]
        vv = rows_ref[pl.ds(k0, tk), 192:256]
        for hs in groups:
            s3 = _mm_nt(_stack_rows(nqr_ref[hs]), kk).reshape(Hg, tq, tk)
            m, l, a = _softmax_update(s3, mask, m_ref[hs], l_ref[hs], acc_ref[hs],
                                      lambda w: _mm(_stack_rows(w), vv).reshape(Hg, tq, HEAD_DIM))
            m_ref[hs] = m
            l_ref[hs] = l
            acc_ref[hs] = a
        return carry

    lax.fori_loop(0, (q0 + tq - 1) // tk + 1, sel_step, 0)

    w0 = pl.multiple_of(jnp.clip(q0 - WINDOW, 0, T - wlen), 8)
    kpos = w0 + lax.broadcasted_iota(jnp.int32, (1, wlen), 1)
    mask_w = ((kpos > qpos - WINDOW) & (kpos <= qpos))[None]
    kw = win_ref[pl.ds(w0, wlen), 0:HEAD_DIM]
    vw = win_ref[pl.ds(w0, wlen), HEAD_DIM:]
    for g, hs in enumerate(groups):
        o = out[g] + gate(1, hs) * (acc_ref[hs] / jnp.maximum(l_ref[hs], 1e-30))
        p = _masked_softmax(_mm_nt(_stack_rows(nqr_ref[hs]), kw).reshape(Hg, tq, wlen), mask_w)
        o = o + gate(2, hs) * _mm(_stack_rows(p), vw).reshape(Hg, tq, HEAD_DIM)
        for i in range(Hg):
            h = hs.start + i
            o_ref[:, h * HEAD_DIM:(h + 1) * HEAD_DIM] = o[i]


def _nsa_prompt(nq, nqr, small, nsa_rows, win_rows, kcv, selw, et, B, T, tq):
    nqt = T // tq
    tk = _key_tile(T)
    wlen = min(T, WINDOW + tq)
    assert (T - wlen) % 8 == 0 and T % tk == 0
    n_sel = -(-T // L_SEL)
    k_top = min(N_SELECT, n_sel)
    nch = kcv.shape[1]
    head = pl.BlockSpec((N_HEADS, tq, HEAD_DIM), lambda b, i: (0, b * nqt + i, 0))
    return pl.pallas_call(
        functools.partial(_nsa_prompt_body, tq=tq, tk=tk, wlen=wlen, k_top=k_top),
        grid=(B, nqt),
        in_specs=[head, head,
                  pl.BlockSpec((tq, 128), lambda b, i: (b * nqt + i, 0)),
                  pl.BlockSpec((T, 256), lambda b, i: (b, 0)),
                  pl.BlockSpec((T, 128), lambda b, i: (b, 0)),
                  pl.BlockSpec((None, nch, 128), lambda b, i: (b, 0, 0)),
                  pl.BlockSpec(selw.shape, lambda b, i: (0, 0)),
                  pl.BlockSpec(et.shape, lambda b, i: (0, 0))],
        out_specs=pl.BlockSpec((tq, N_HEADS * HEAD_DIM), lambda b, i: (b * nqt + i, 0)),
        out_shape=jax.ShapeDtypeStruct((B * T, N_HEADS * HEAD_DIM), F32),
        scratch_shapes=[pltpu.VMEM((N_HEADS, tq, 1), F32), pltpu.VMEM((N_HEADS, tq, 1), F32),
                        pltpu.VMEM((N_HEADS, tq, HEAD_DIM), F32)],
        compiler_params=_cp("parallel", "arbitrary"),
        name="nsa_prompt",
    )(nq, nqr, small, nsa_rows, win_rows, kcv, selw, et)


def _cumsum_body(small_ref, cst_ref, *, S, tc):
    r = lax.broadcasted_iota(jnp.int32, (tc, tc), 0)
    c = lax.broadcasted_iota(jnp.int32, (tc, tc), 1)
    upper = (r <= c).astype(MXU_DTYPE)

    def step(i, carry):
        k0 = pl.multiple_of(i * tc, tc)
        xt = small_ref[pl.ds(k0, tc), :].T
        ct = _mm3(xt, upper) + carry
        cst_ref[i] = ct[S_LOGF:S_LOGF + N_HEADS, :]
        return ct[:, tc - 1:tc]

    lax.fori_loop(0, S // tc, step, jnp.zeros((128, 1), F32))


def _fox_cumsum(small, B, S, tc):
    return pl.pallas_call(
        functools.partial(_cumsum_body, S=S, tc=tc),
        name="forget_cumsum",
        grid=(B,),
        in_specs=[pl.BlockSpec((S, 128), lambda b: (b, 0))],
        out_specs=pl.BlockSpec((None, S // tc, N_HEADS, tc), lambda b: (b, 0, 0, 0)),
        out_shape=jax.ShapeDtypeStruct((B, S // tc, N_HEADS, tc), F32),
        compiler_params=_cp("parallel"),
    )(small)


def _flash_body(*refs, tq, tk, dv, per_head, decay, scale):
    if decay:
        q_ref, k_ref, c_ref, o_ref, m_ref, l_ref, acc_ref = refs
        v_ref = None
    elif per_head:
        q_ref, k_ref, v_ref, o_ref, m_ref, l_ref, acc_ref = refs
    else:
        q_ref, k_ref, o_ref, m_ref, l_ref, acc_ref = refs
        v_ref = None
    H = N_HEADS
    qi = pl.program_id(1)
    kj = pl.program_id(2)
    q0 = qi * tq
    k0 = kj * tk

    @pl.when(kj == 0)
    def _():
        m_ref[...] = jnp.full(m_ref.shape, NEG, F32)
        l_ref[...] = jnp.zeros(l_ref.shape, F32)
        acc_ref[...] = jnp.zeros(acc_ref.shape, F32)

    @pl.when(k0 <= q0 + tq - 1)
    def _():
        qpos = q0 + lax.broadcasted_iota(jnp.int32, (tq, 1), 0)
        kpos = k0 + lax.broadcasted_iota(jnp.int32, (1, tk), 1)
        mask = (kpos <= qpos)[None]
        Hg = H // HEAD_GROUPS
        for g in range(HEAD_GROUPS):
            hs = slice(g * Hg, (g + 1) * Hg)
            q = q_ref[hs]
            if per_head:
                s3 = lax.dot_general(q, k_ref[hs], (((2,), (2,)), ((0,), (0,))), preferred_element_type=F32)
                pv = lambda w, hs=hs: lax.dot_general(w.astype(MXU_DTYPE), v_ref[hs], (((2,), (1,)), ((0,), (0,))),
                                               preferred_element_type=F32)
            else:
                vv = k_ref[:, HEAD_DIM:2 * HEAD_DIM]
                s3 = _mm_nt(_stack_rows(q), k_ref[:, 0:HEAD_DIM]).reshape(Hg, tq, tk)
                pv = lambda w, vv=vv: _mm(_stack_rows(w), vv).reshape(Hg, tq, dv)
            if scale != 1.0:
                s3 = s3 * scale
            if decay:
                s3 = s3 - c_ref[hs][:, None, :]
            m, l, a = _softmax_update(s3, mask, m_ref[hs], l_ref[hs], acc_ref[hs], pv)
            m_ref[hs] = m
            l_ref[hs] = l
            acc_ref[hs] = a

    @pl.when(kj == pl.num_programs(2) - 1)
    def _():
        o = acc_ref[...] / jnp.maximum(l_ref[...], 1e-30)
        for h in range(H):
            o_ref[:, h * dv:(h + 1) * dv] = o[h]


def _flash_prompt(q, k, v, cst, B, T, tq, tk, *, scale):
    nqt, nkt = T // tq, T // tk
    d = q.shape[-1]
    per_head = v is not None
    decay = cst is not None
    last = lambda i: (i * tq + tq - 1) // tk
    in_specs = [pl.BlockSpec((N_HEADS, tq, d), lambda b, i, j: (0, b * nqt + i, 0))]
    args = [q]
    if per_head:
        dv = v.shape[-1]
        in_specs += [pl.BlockSpec((None, N_HEADS, tk, d), lambda b, i, j: (b, 0, jnp.minimum(j, last(i)), 0)),
                     pl.BlockSpec((None, N_HEADS, tk, dv), lambda b, i, j: (b, 0, jnp.minimum(j, last(i)), 0))]
        args += [k, v]
    else:
        dv = HEAD_DIM
        in_specs += [pl.BlockSpec((tk, 128), lambda b, i, j: (b * nkt + jnp.minimum(j, last(i)), 0))]
        args += [k]
    if decay:
        assert cst.shape[-1] == tk
        in_specs += [pl.BlockSpec((None, None, N_HEADS, tk), lambda b, i, j: (b, jnp.minimum(j, last(i)), 0, 0))]
        args += [cst]
    return pl.pallas_call(
        functools.partial(_flash_body, tq=tq, tk=tk, dv=dv, per_head=per_head, decay=decay, scale=scale),
        grid=(B, nqt, nkt),
        in_specs=in_specs,
        out_specs=pl.BlockSpec((tq, N_HEADS * dv), lambda b, i, j: (b * nqt + i, 0)),
        out_shape=jax.ShapeDtypeStruct((B * T, N_HEADS * dv), F32),
        scratch_shapes=[pltpu.VMEM((N_HEADS, tq, 1), F32), pltpu.VMEM((N_HEADS, tq, 1), F32),
                        pltpu.VMEM((N_HEADS, tq, dv), F32)],
        compiler_params=_cp("parallel", "parallel", "arbitrary"),
        name="flash_latent" if per_head else "flash_forget",
    )(*args)


def _mla_kv_body(r_ref, wuk_ref, wuv_ref, gkn_ref, gkr_ref, g64_ref, k_ref, v_ref):
    ckv = r_ref[:, 0:KV_LORA]
    kr = r_ref[:, KV_LORA:KV_LORA + MLA_ROPE]
    kn = _mm(ckv, wuk_ref[...])
    ss = _mm3(kn * kn, g64_ref[...]) + jnp.sum(kr * kr, axis=-1, keepdims=True)
    inv = lax.rsqrt(ss / MLA_QK + EPS)
    knn = kn * gkn_ref[...] * inv
    krg = kr * gkr_ref[...]
    vv = _mm(ckv, wuv_ref[...])
    for h in range(N_HEADS):
        sl = slice(h * HEAD_DIM, (h + 1) * HEAD_DIM)
        k_ref[h, :, 0:MLA_NOPE] = knn[:, sl].astype(k_ref.dtype)
        k_ref[h, :, MLA_NOPE:MLA_QK] = (krg * inv[:, h * HEAD_DIM:h * HEAD_DIM + MLA_ROPE]).astype(k_ref.dtype)
        v_ref[h] = vv[:, sl].astype(v_ref.dtype)


def _mla_kv(mla_rows, B, T, wuk, wuv, gkn, gkr, g64, tk):
    nkt = T // tk
    full = lambda a: pl.BlockSpec(a.shape, lambda b, j: (0,) * a.ndim)
    return pl.pallas_call(
        _mla_kv_body,
        name="latent_kv",
        grid=(B, nkt),
        in_specs=[pl.BlockSpec((tk, KV_LORA + MLA_ROPE), lambda b, j: (b * nkt + j, 0)),
                  full(wuk), full(wuv), full(gkn), full(gkr), full(g64)],
        out_specs=[pl.BlockSpec((None, N_HEADS, tk, MLA_QK), lambda b, j: (b, 0, j, 0)),
                   pl.BlockSpec((None, N_HEADS, tk, HEAD_DIM), lambda b, j: (b, 0, j, 0))],
        out_shape=[jax.ShapeDtypeStruct((B, N_HEADS, T, MLA_QK), MXU_DTYPE),
                   jax.ShapeDtypeStruct((B, N_HEADS, T, HEAD_DIM), MXU_DTYPE)],
        compiler_params=_cp("parallel", "parallel"),
    )(mla_rows, wuk, wuv, gkn, gkr, g64)


def _merge_body(x_ref, o1_ref, o2_ref, o3_ref, g_ref, wb_ref, wo_ref, out_ref):
    d = x_ref.shape[1]
    acc = jnp.zeros(x_ref.shape, F32)
    for b, o_ref in enumerate((o1_ref, o2_ref, o3_ref)):
        acc = acc + jax.nn.sigmoid(g_ref[:, b * d:(b + 1) * d]) * _mm(o_ref[...], wb_ref[b])
    out_ref[...] = x_ref[...] + _mm(acc, wo_ref[...])


def _merge(x, o1, o2, o3, g, wb, wo, tm):
    M, D = x.shape
    row = lambda w: pl.BlockSpec((tm, w), lambda i: (i, 0))
    return pl.pallas_call(
        _merge_body,
        name="merge",
        grid=(M // tm,),
        in_specs=[row(D), row(o1.shape[1]), row(o2.shape[1]), row(o3.shape[1]), row(g.shape[1]),
                  pl.BlockSpec(wb.shape, lambda i: (0, 0, 0)), pl.BlockSpec(wo.shape, lambda i: (0, 0))],
        out_specs=row(D),
        out_shape=jax.ShapeDtypeStruct((M, D), F32),
        compiler_params=_cp("parallel"),
    )(x, o1, o2, o3, g, wb, wo)


def _mem_attn_body(x_ref, kv_ref, g_ref, wq_ref, gq_ref, wo_ref, o_ref):
    x = x_ref[...]
    q = _mm(_rms(x, g_ref[...]), wq_ref[...])
    outs = []
    for h in range(N_MEM_HEADS):
        sl = slice(h * MEM_HD, (h + 1) * MEM_HD)
        qh = _rms(q[:, sl], gq_ref[...])
        s = _mm_nt(qh, kv_ref[:, sl]) * MEM_HD ** -0.5
        p = jnp.exp(s - jnp.max(s, axis=-1, keepdims=True))
        p = p / jnp.sum(p, axis=-1, keepdims=True)
        outs.append(_mm(p, kv_ref[:, MEM_W + h * MEM_HD:MEM_W + (h + 1) * MEM_HD]))
    o_ref[...] = x + _mm(jnp.concatenate(outs, axis=1), wo_ref[...])


def _mem_attn(x, mem_kv, g, wq, gq, wo, B, T, tm):
    D = x.shape[1]
    nt = T // tm
    n_mem = mem_kv.shape[1]
    full = lambda a: pl.BlockSpec(a.shape, lambda b, i: (0,) * a.ndim)
    return pl.pallas_call(
        _mem_attn_body,
        name="mem_attn",
        grid=(B, nt),
        in_specs=[pl.BlockSpec((tm, D), lambda b, i: (b * nt + i, 0)),
                  pl.BlockSpec((None, n_mem, 2 * MEM_W), lambda b, i: (b, 0, 0)),
                  full(g), full(wq), full(gq), full(wo)],
        out_specs=pl.BlockSpec((tm, D), lambda b, i: (b * nt + i, 0)),
        out_shape=jax.ShapeDtypeStruct(x.shape, F32),
        compiler_params=_cp("parallel", "parallel"),
    )(x, mem_kv, g, wq, gq, wo)


def _mlp_body(x_ref, g_ref, wu_ref, wd_ref, o_ref, h_ref, acc_ref):
    j = pl.program_id(1)

    @pl.when(j == 0)
    def _():
        h_ref[...] = _rms(x_ref[...], g_ref[...]).astype(h_ref.dtype)
        acc_ref[...] = jnp.zeros(acc_ref.shape, F32)

    u = jnp.dot(h_ref[...], wu_ref[...], preferred_element_type=F32)
    acc_ref[...] += _mm(jnp.square(jnp.maximum(u, 0.0)), wd_ref[...])

    @pl.when(j == pl.num_programs(1) - 1)
    def _():
        o_ref[...] = x_ref[...] + acc_ref[...]


def _mlp(x, g, wu, wd, tm, tf):
    M, D = x.shape
    FF = wu.shape[1]
    return pl.pallas_call(
        _mlp_body,
        name="mlp",
        grid=(M // tm, FF // tf),
        in_specs=[pl.BlockSpec((tm, D), lambda i, j: (i, 0)),
                  pl.BlockSpec((1, D), lambda i, j: (0, 0)),
                  pl.BlockSpec((D, tf), lambda i, j: (0, j)),
                  pl.BlockSpec((tf, D), lambda i, j: (j, 0))],
        out_specs=pl.BlockSpec((tm, D), lambda i, j: (i, 0)),
        out_shape=jax.ShapeDtypeStruct((M, D), F32),
        scratch_shapes=[pltpu.VMEM((tm, D), MXU_DTYPE), pltpu.VMEM((tm, D), F32)],
        compiler_params=_cp("parallel", "arbitrary"),
    )(x, g, wu, wd)


def _page_copies(pt_ref, bb, slot, n_pages, page, pairs, start):
    def body(p, carry):
        pg = pt_ref[bb, p]
        for src_fn, dst_fn, sem in pairs:
            cp = pltpu.make_async_copy(src_fn(pg), dst_fn(slot, p), sem.at[slot])
            if start:
                cp.start()
            else:
                cp.wait()
        return carry

    lax.fori_loop(0, n_pages, body, 0)


def _gather_pipeline(pt_ref, n_pages, page, pairs):
    b = pl.program_id(0)
    nb = pl.num_programs(0)

    @pl.when(b == 0)
    def _():
        _page_copies(pt_ref, 0, 0, n_pages, page, pairs, True)

    @pl.when(b + 1 < nb)
    def _():
        _page_copies(pt_ref, b + 1, (b + 1) % 2, n_pages, page, pairs, True)

    slot = b % 2
    _page_copies(pt_ref, b, slot, n_pages, page, pairs, False)
    return slot


def _row_mod(rows, T):
    return lax.broadcasted_iota(jnp.int32, (rows, 1), 0) % T


def _heads_to_lanes(o_ref, o, T, dv):
    for h in range(N_HEADS):
        o_ref[:, h * dv:(h + 1) * dv] = o[h * T:(h + 1) * T, :]


def _fox_sample_body(pt_ref, q_ref, newkv_ref, newsmall_ref, lstrict_ref, kv_hbm, lf_hbm, o_ref, kvbuf, lfbuf, ct_ref,
                     sem_kv, sem_lf, *, layer, n_pages, page, tk, T):
    pairs = [(lambda pg: kv_hbm.at[layer, pg], lambda s, p: kvbuf.at[s, pl.ds(p * page, page)], sem_kv),
             (lambda pg: lf_hbm.at[layer, pg], lambda s, p: lfbuf.at[s, p], sem_lf)]
    slot = _gather_pipeline(pt_ref, n_pages, page, pairs)
    H = N_HEADS
    R = H * T
    P = n_pages * page

    upper = (lax.broadcasted_iota(jnp.int32, (page, page), 0)
             <= lax.broadcasted_iota(jnp.int32, (page, page), 1)).astype(MXU_DTYPE)
    local = _mm3(lfbuf[slot].reshape(n_pages * H, page), upper)
    tot = jnp.broadcast_to(local[:, page - 1:page], local.shape)
    ct_all = local + _mm3_lhs_exact(lstrict_ref[...], tot)
    ct_ref[...] = ct_all.reshape(n_pages, H, page)
    total = ct_all[(n_pages - 1) * H:, page - 1:page]
    eye = (lax.broadcasted_iota(jnp.int32, (H, H), 0) == lax.broadcasted_iota(jnp.int32, (H, H), 1)).astype(MXU_DTYPE)
    upper_t = (lax.broadcasted_iota(jnp.int32, (T, T), 0) <= lax.broadcasted_iota(jnp.int32, (T, T), 1)).astype(MXU_DTYPE)
    ct_new = _mm3(_mm3_nt_lhs_exact(eye, newsmall_ref[:, S_LOGF:S_LOGF + H]), upper_t) + total

    q = _stack_rows(q_ref[...])
    ppt = tk // page

    def step(j, carry):
        m, l, acc = carry
        k0 = pl.multiple_of(j * tk, tk)
        kv = kvbuf[slot, pl.ds(k0, tk), :]
        ct = jnp.concatenate([ct_ref[j * ppt + i] for i in range(ppt)], axis=1)
        s = (_mm_nt(q, kv[:, 0:HEAD_DIM]).reshape(H, T, tk) - ct[:, None, :]).reshape(R, tk)
        return _softmax_update(s, None, m, l, acc, kv[:, HEAD_DIM:])

    init = (jnp.full((R, 1), NEG, F32), jnp.zeros((R, 1), F32), jnp.zeros((R, HEAD_DIM), F32))
    m, l, acc = lax.fori_loop(0, P // tk, step, init, unroll=2)

    knew = newkv_ref[...]
    s = (_mm_nt(q, knew[:, 0:HEAD_DIM]).reshape(H, T, T) - ct_new[:, None, :]).reshape(R, T)
    mask = lax.broadcasted_iota(jnp.int32, (1, T), 1) <= _row_mod(R, T)
    m, l, acc = _softmax_update(s, mask, m, l, acc, knew[:, HEAD_DIM:])
    _heads_to_lanes(o_ref, acc / jnp.maximum(l, 1e-30), T, HEAD_DIM)


def _fox_sample(page_table, fq, fox_rows, small, lstrict, cache_kv, cache_lf_t, layer, B, T):
    n_pages = page_table.shape[1]
    page = cache_kv.shape[2]
    P = n_pages * page
    tk = min(2048, P)
    grid_spec = pltpu.PrefetchScalarGridSpec(
        num_scalar_prefetch=1,
        grid=(B,),
        in_specs=[pl.BlockSpec((N_HEADS, T, HEAD_DIM), lambda b, pt: (0, b, 0)),
                  pl.BlockSpec((T, 128), lambda b, pt: (b, 0)),
                  pl.BlockSpec((T, 128), lambda b, pt: (b, 0)),
                  pl.BlockSpec(lstrict.shape, lambda b, pt: (0, 0)),
                  pl.BlockSpec(memory_space=pl.ANY),
                  pl.BlockSpec(memory_space=pl.ANY)],
        out_specs=pl.BlockSpec((T, N_HEADS * HEAD_DIM), lambda b, pt: (b, 0)),
        scratch_shapes=[pltpu.VMEM((2, P, 128), F32),
                        pltpu.VMEM((2, n_pages, N_HEADS, page), F32),
                        pltpu.VMEM((n_pages, N_HEADS, page), F32),
                        pltpu.SemaphoreType.DMA((2,)), pltpu.SemaphoreType.DMA((2,))])
    return pl.pallas_call(
        functools.partial(_fox_sample_body, layer=layer, n_pages=n_pages, page=page, tk=tk, T=T),
        grid_spec=grid_spec,
        out_shape=jax.ShapeDtypeStruct((B * T, N_HEADS * HEAD_DIM), F32),
        compiler_params=_cp("arbitrary"),
        name="forget_sample",
    )(page_table, fq, fox_rows, small, lstrict, cache_kv, cache_lf_t)


def _mla_sample_body(pt_ref, q_ref, new_ref, wukt_ref, wuv_ref, gkn_ref, gkr_ref, c_hbm, o_ref, buf, sem,
                     *, layer, n_pages, page, ppt, T):
    pairs = [(lambda pg: c_hbm.at[layer, pg], lambda s, p: buf.at[s, p], sem)]
    slot = _gather_pipeline(pt_ref, n_pages, page, pairs)
    H = N_HEADS
    R = H * T
    tk = ppt * page
    scale = MLA_QK ** -0.5

    ql, qr = [], []
    for h in range(H):
        qh = q_ref[h].astype(F32)
        ql.append(_mm(qh[:, 0:MLA_NOPE] * gkn_ref[...], wukt_ref[h]))
        qr.append(qh[:, MLA_NOPE:MLA_QK] * gkr_ref[...])
    ql = jnp.concatenate(ql, axis=0)
    qr = jnp.concatenate(qr, axis=0)
    qcat = jnp.concatenate([ql, qr], axis=1).astype(MXU_DTYPE)
    wukt = wukt_ref[...].reshape(H * MLA_NOPE, KV_LORA)

    def key_scale(knt, kr2, n):
        ssq = jnp.sum((knt * knt).reshape(H, MLA_NOPE, n), axis=1) + kr2
        return lax.rsqrt(ssq / MLA_QK + EPS) * scale

    def step(j, carry):
        m, l, acc = carry
        pt = jnp.concatenate([buf[slot, j * ppt + i] for i in range(ppt)], axis=1)
        krt = pt[KV_LORA:KV_LORA + MLA_ROPE]
        ptm = pt.astype(MXU_DTYPE)
        ckvt = ptm[0:KV_LORA]
        inv = key_scale(jnp.dot(wukt, ckvt, preferred_element_type=F32), jnp.sum(krt * krt, axis=0, keepdims=True), tk)
        s = jnp.dot(qcat, ptm, preferred_element_type=F32)
        s = (s.reshape(H, T, tk) * inv[:, None, :]).reshape(R, tk)
        return _softmax_update(s, None, m, l, acc, lambda w: _mm_nt(w, ckvt))

    init = (jnp.full((R, 1), NEG, F32), jnp.zeros((R, 1), F32), jnp.zeros((R, KV_LORA), F32))
    m, l, acc = lax.fori_loop(0, n_pages // ppt, step, init, unroll=2)

    rows = new_ref[...]
    ckv = rows[:, 0:KV_LORA]
    kr = rows[:, KV_LORA:KV_LORA + MLA_ROPE]
    inv = key_scale(_mm_nt(wukt, ckv), _mm3_nt_lhs_exact(jnp.ones((H, MLA_ROPE), MXU_DTYPE), kr * kr), T)
    s = (_mm_nt(qcat, rows).reshape(H, T, T) * inv[:, None, :]).reshape(R, T)
    mask = lax.broadcasted_iota(jnp.int32, (1, T), 1) <= _row_mod(R, T)
    m, l, acc = _softmax_update(s, mask, m, l, acc, ckv)
    lat = acc / jnp.maximum(l, 1e-30)
    for h in range(H):
        o_ref[:, h * HEAD_DIM:(h + 1) * HEAD_DIM] = _mm(lat[h * T:(h + 1) * T, :], wuv_ref[h])


def _mla_sample(page_table, mq, mla_rows, wukt, wuv, gkn, gkr, cache_t, layer, B, T):
    n_pages = page_table.shape[1]
    F, page = cache_t.shape[2], cache_t.shape[3]
    ppt = min(8, n_pages)
    assert n_pages % ppt == 0
    full = lambda a: pl.BlockSpec(a.shape, lambda b, pt: (0,) * a.ndim)
    grid_spec = pltpu.PrefetchScalarGridSpec(
        num_scalar_prefetch=1,
        grid=(B,),
        in_specs=[pl.BlockSpec((N_HEADS, T, MLA_QK), lambda b, pt: (0, b, 0)),
                  pl.BlockSpec((T, F), lambda b, pt: (b, 0)),
                  full(wukt), full(wuv), full(gkn), full(gkr),
                  pl.BlockSpec(memory_space=pl.ANY)],
        out_specs=pl.BlockSpec((T, N_HEADS * HEAD_DIM), lambda b, pt: (b, 0)),
        scratch_shapes=[pltpu.VMEM((2, n_pages, F, page), F32), pltpu.SemaphoreType.DMA((2,))])
    return pl.pallas_call(
        functools.partial(_mla_sample_body, layer=layer, n_pages=n_pages, page=page, ppt=ppt, T=T),
        grid_spec=grid_spec,
        out_shape=jax.ShapeDtypeStruct((B * T, N_HEADS * HEAD_DIM), F32),
        compiler_params=_cp("arbitrary"),
        name="latent_sample",
    )(page_table, mq, mla_rows, wukt, wuv, gkn, gkr, cache_t)


def _nsa_sample_body(pt_ref, nq_ref, nqr_ref, small_ref, newrows_ref, newwin_ref, pastwin_ref, pe_ref, w_ref, g_ref,
                     selw_ref, et_ref, c_hbm, o_ref, winout_ref, bufc, bufs, sem_c, sem_s, *, layer, n_pages, page, tk,
                     T, k_top):
    pairs = [(lambda pg: c_hbm.at[layer, pg, :, pl.ds(0, 128)], lambda s, p: bufc.at[s, pl.ds(p * page, page)], sem_c),
             (lambda pg: c_hbm.at[layer, pg, :, pl.ds(128, 128)], lambda s, p: bufs.at[s, pl.ds(p * page, page)], sem_s)]
    slot = _gather_pipeline(pt_ref, n_pages, page, pairs)
    R = N_HEADS * T
    P = n_pages * page
    nch = P // STRIDE_CMP
    nsp = selw_ref.shape[1]
    wb = pastwin_ref.shape[0]
    qidx = _row_mod(R, T)
    qpos = P + qidx
    qn = nq_ref[...].reshape(R, HEAD_DIM)
    qr = nqr_ref[...].reshape(R, HEAD_DIM)
    new_mask = lax.broadcasted_iota(jnp.int32, (1, T), 1) <= qidx
    init = (jnp.full((R, 1), NEG, F32), jnp.zeros((R, 1), F32), jnp.zeros((R, HEAD_DIM), F32))

    kcv = _compress(lambda l: bufc[slot, pl.ds(l, nch, stride=STRIDE_CMP), :], nch, pe_ref, w_ref, g_ref)
    cmp_end = lax.broadcasted_iota(jnp.int32, (1, nch), 1) * STRIDE_CMP + (L_CMP - 1)
    mask_c = cmp_end <= qpos
    s = jnp.where(mask_c, _mm_nt(qn, kcv[:, 0:HEAD_DIM]), NEG)
    p = jnp.where(mask_c, jnp.exp(s - jnp.max(s, axis=-1, keepdims=True)), 0.0)
    p = p / jnp.maximum(jnp.sum(p, axis=-1, keepdims=True), 1e-30)
    o_c = _mm(p, kcv[:, HEAD_DIM:])
    imp = _mm3(jnp.sum(p.reshape(N_HEADS, T, nch), axis=0), selw_ref[...])

    ids = lax.broadcasted_iota(jnp.int32, (1, nsp), 1)
    qp = P + lax.broadcasted_iota(jnp.int32, (T, 1), 0)
    cur = qp // L_SEL
    forced = (ids == 0) | (ids == cur) | (ids == cur - 1)
    score = jnp.where(forced, jnp.inf, jnp.where(ids * L_SEL <= qp, imp, -jnp.inf))
    selm = _topk_mask(score, -(-(P + T) // L_SEL), k_top).astype(MXU_DTYPE)
    selm = jnp.concatenate([selm] * N_HEADS, axis=0)

    def sel_step(j, carry):
        m, l, acc = carry
        k0 = pl.multiple_of(j * tk, tk)
        blk = _mm_nt(selm, et_ref[pl.ds(k0, tk), :]) > 0.5
        return _softmax_update(_mm_nt(qr, bufs[slot, pl.ds(k0, tk), 0:HEAD_DIM]), blk, m, l, acc,
                               bufs[slot, pl.ds(k0, tk), HEAD_DIM:])

    m, l, acc = lax.fori_loop(0, P // tk, sel_step, init, unroll=2)
    blk = (_mm_nt(selm, et_ref[pl.ds(P, T), :]) > 0.5) & new_mask
    m, l, acc = _softmax_update(_mm_nt(qr, newrows_ref[:, 128:192]), blk, m, l, acc, newrows_ref[:, 192:256])
    o_s = acc / jnp.maximum(l, 1e-30)

    mask_w = lax.broadcasted_iota(jnp.int32, (1, wb), 1) > qidx + (wb - WINDOW)
    m, l, acc = _softmax_update(_mm_nt(qr, pastwin_ref[:, 0:HEAD_DIM]), mask_w, *init, pastwin_ref[:, HEAD_DIM:])
    m, l, acc = _softmax_update(_mm_nt(qr, newwin_ref[:, 0:HEAD_DIM]), new_mask, m, l, acc, newwin_ref[:, HEAD_DIM:])
    o_w = acc / jnp.maximum(l, 1e-30)

    gates = [jnp.concatenate([small_ref[:, S_GATE + 3 * h + j:S_GATE + 3 * h + j + 1] for h in range(N_HEADS)], axis=0)
             for j in range(3)]
    _heads_to_lanes(o_ref, gates[0] * o_c + gates[1] * o_s + gates[2] * o_w, T, HEAD_DIM)

    keep = winout_ref.shape[0] - T
    winout_ref[0:keep, :] = pastwin_ref[wb - keep:wb, :]
    winout_ref[keep:keep + T, :] = newwin_ref[...]


def _nsa_sample(page_table, nq, nqr, small, nsa_rows, win_rows, state_win, pe, wblk, gk, selw, et, cache, layer, B, T):
    n_pages = page_table.shape[1]
    page = cache.shape[2]
    P = n_pages * page
    wb = state_win.shape[2]
    assert T < STRIDE_CMP and P % STRIDE_CMP == 0 and P + T >= WINDOW and wb >= WINDOW - T
    tk = min(2048, P)
    k_top = min(N_SELECT, -(-(P + T) // L_SEL))
    full = lambda a: pl.BlockSpec(a.shape, lambda b, pt: (0,) * a.ndim)
    head = pl.BlockSpec((N_HEADS, T, HEAD_DIM), lambda b, pt: (0, b, 0))
    grid_spec = pltpu.PrefetchScalarGridSpec(
        num_scalar_prefetch=1,
        grid=(B,),
        in_specs=[head, head,
                  pl.BlockSpec((T, 128), lambda b, pt: (b, 0)),
                  pl.BlockSpec((T, 256), lambda b, pt: (b, 0)),
                  pl.BlockSpec((T, 128), lambda b, pt: (b, 0)),
                  pl.BlockSpec((None, None, wb, 128), lambda b, pt: (layer, b, 0, 0)),
                  full(pe), full(wblk), full(gk), full(selw), full(et),
                  pl.BlockSpec(memory_space=pl.ANY)],
        out_specs=[pl.BlockSpec((T, N_HEADS * HEAD_DIM), lambda b, pt: (b, 0)),
                   pl.BlockSpec((None, WINDOW, 128), lambda b, pt: (b, 0, 0))],
        scratch_shapes=[pltpu.VMEM((2, P, 128), F32), pltpu.VMEM((2, P, 128), F32),
                        pltpu.SemaphoreType.DMA((2,)), pltpu.SemaphoreType.DMA((2,))])
    return pl.pallas_call(
        functools.partial(_nsa_sample_body, layer=layer, n_pages=n_pages, page=page, tk=tk, T=T, k_top=k_top),
        grid_spec=grid_spec,
        out_shape=[jax.ShapeDtypeStruct((B * T, N_HEADS * HEAD_DIM), F32),
                   jax.ShapeDtypeStruct((B, WINDOW, 128), F32)],
        compiler_params=_cp("arbitrary"),
        name="nsa_sample",
    )(page_table, nq, nqr, small, nsa_rows, win_rows, state_win, pe, wblk, gk, selw, et, cache)


def _block_ones(n, g):
    i = np.arange(n)
    return jnp.asarray(i[:, None] // g == i[None, :] // g, MXU_DTYPE)


def _sel_tables(S, nch):
    n_cmp = (S - L_CMP) // STRIDE_CMP + 1
    n_sel = -(-S // L_SEL)
    nsp = -(-n_sel // 128) * 128
    cs = np.arange(nch)[:, None] * STRIDE_CMP
    ss = np.arange(nsp)[None, :] * L_SEL
    w = np.maximum(np.minimum(cs + L_CMP, ss + L_SEL) - np.maximum(cs, ss), 0) // STRIDE_CMP
    w = np.where((np.arange(nch)[:, None] < n_cmp) & (np.arange(nsp)[None, :] < n_sel), w, 0)
    s_pad = -(-S // 16) * 16
    et = np.arange(s_pad)[:, None] // L_SEL == np.arange(nsp)[None, :]
    return jnp.asarray(w, MXU_DTYPE), jnp.asarray(et, MXU_DTYPE), n_sel


def _row_tile(M, pref):
    return pref if M % pref == 0 else M


def _key_tile(T):
    return next(t for t in (1024, 512, 256, 128, T) if T % t == 0)


def kernel(x_prompt, x_sample, cache_nsa, cache_fox_kv, cache_fox_logf, cache_mla, state_nsa_win, cache_mem, page_table, mem_prompt, norm_mix, w_in, b_fox, nsa_q_norm, nsa_k_norm, nsa_pe, nsa_w_cmp, fox_q_norm, fox_k_norm, mla_q_norm, mla_kv_norm, mla_w_uk, mla_w_uv, mla_k_norm, w_branch, w_o, norm_mem, mem_in_norm, w_q_mem, w_kv_mem, mem_q_norm, mem_k_norm, w_o_mem, norm_mlp, w_up, w_down):
    B, T, D = x_prompt.shape
    Bs, Ts, _ = x_sample.shape
    depth = w_in.shape[0]
    n_pages = page_table.shape[1]
    page = cache_nsa.shape[2]
    P = n_pages * page
    n_mem = mem_prompt.shape[1]
    bf = lambda a: a.astype(MXU_DTYPE)
    ones = lambda n: jnp.ones((depth, n), F32)

    splits = np.cumsum([0, 512, 384, 24, 512, 128, 8, 768, KV_LORA, MLA_ROPE, N_BRANCH * D])
    col = lambda i: w_in[:, :, splits[i]:splits[i + 1]]
    mla_q = col(6).reshape(depth, D, N_HEADS, MLA_QK)
    small_cols = jnp.concatenate([col(2), col(5), col(8), jnp.zeros((depth, D, 128 - 24 - 8 - MLA_ROPE), F32)], axis=2)
    w_rest = bf(jnp.concatenate([col(0), col(1), col(3), col(4),
                                 mla_q[..., :MLA_NOPE].reshape(depth, D, N_HEADS * MLA_NOPE),
                                 mla_q[..., MLA_NOPE:].reshape(depth, D, N_HEADS * MLA_ROPE),
                                 col(7), small_cols], axis=2))
    w_gate = bf(col(9))
    gqn = jnp.tile(nsa_q_norm, (1, N_HEADS))
    gkv = jnp.concatenate([ones(128), nsa_k_norm[:, 1], ones(64), nsa_k_norm[:, 2], ones(64)], axis=1)
    mkv = jnp.asarray(np.repeat([0, 0, 1, 0, 1, 0], HEAD_DIM)[None, :], F32)
    gqf = jnp.tile(fox_q_norm, (1, N_HEADS))
    gfk = jnp.concatenate([fox_k_norm, ones(64)], axis=1)
    gmn = jnp.tile(mla_q_norm[:, :MLA_NOPE], (1, N_HEADS))
    gmr = jnp.tile(mla_q_norm[:, MLA_NOPE:], (1, N_HEADS))
    bfox = jnp.concatenate([jnp.zeros((depth, S_LOGF), F32), b_fox, jnp.zeros((depth, 128 - S_LOGF - N_HEADS), F32)], axis=1)
    g64 = _block_ones(512, HEAD_DIM)
    g32 = _block_ones(256, MLA_ROPE)
    gx = jnp.asarray(np.arange(256)[:, None] // MLA_ROPE == np.arange(512)[None, :] // HEAD_DIM, MXU_DTYPE)
    gxt = gx.T
    pe = jnp.concatenate([nsa_pe[:, 0], nsa_pe[:, 1]], axis=2)
    zero = jnp.zeros_like(nsa_w_cmp[:, 0])
    wblk = bf(jnp.concatenate([jnp.concatenate([nsa_w_cmp[:, 0], zero], axis=3),
                               jnp.concatenate([zero, nsa_w_cmp[:, 1]], axis=3)], axis=2))
    gkc = jnp.concatenate([nsa_k_norm[:, 0], ones(64)], axis=1)
    wuk = bf(mla_w_uk.reshape(depth, KV_LORA, N_HEADS * MLA_NOPE))
    wuv = bf(mla_w_uv.reshape(depth, KV_LORA, N_HEADS * HEAD_DIM))
    wukt = bf(jnp.transpose(mla_w_uk, (0, 2, 3, 1)))
    wuvh = bf(jnp.transpose(mla_w_uv, (0, 2, 1, 3)))
    gkn = jnp.tile(mla_k_norm[:, :MLA_NOPE], (1, N_HEADS))
    wb_, wo_ = bf(w_branch), bf(w_o)
    wqm, wkvm, wom = bf(w_q_mem), bf(w_kv_mem), bf(w_o_mem)
    wu, wd = bf(w_up), bf(w_down)

    tm_post_p = _row_tile(T, 256)
    tm_post_s = _row_tile(Bs * Ts, 256)
    tab_p = _rope_tables(jnp.arange(T).astype(F32))
    tab_s = jnp.tile(_rope_tables((P + jnp.arange(Ts)).astype(F32)), (tm_post_s // Ts, 1))
    selw_p, et_p, _ = _sel_tables(T, T // STRIDE_CMP)
    selw_s, et_s, _ = _sel_tables(P + Ts, P // STRIDE_CMP)

    xp = x_prompt.reshape(B * T, D)
    xs = x_sample.reshape(Bs * Ts, D)
    tm_p = _row_tile(B * T, 512)
    tm_s = _row_tile(Bs * Ts, 512)
    tq = _row_tile(T, 128)
    tkf = _key_tile(T)
    cache_mla_t = jnp.swapaxes(cache_mla, 2, 3)
    cache_lf_t = jnp.swapaxes(cache_fox_logf, 2, 3)
    pr = np.arange(n_pages * N_HEADS)
    lstrict = jnp.asarray((pr[None, :] % N_HEADS == pr[:, None] % N_HEADS) & (pr[None, :] // N_HEADS < pr[:, None] // N_HEADS),
                          MXU_DTYPE)
    outs = [[] for _ in range(11)]
    r2 = lambda a: a[None, :]

    for l in range(depth):
        post_consts = [r2(gqn[l]), r2(gkv[l]), mkv, r2(gqf[l]), r2(gfk[l]), r2(gmn[l]), r2(gmr[l]),
                       r2(mla_kv_norm[l]), r2(bfox[l]), g64, gx, gxt, g32]
        cmp_consts = (pe[l], wblk[l], r2(gkc[l]))

        def mixers_in(x, tm, tab, Tg, tm_post):
            g = _norm_matmul(x, r2(norm_mix[l]), w_gate[l], tm)
            p = _norm_matmul(x, r2(norm_mix[l]), w_rest[l], tm)
            return (g,) + tuple(_post(p, tab, Tg, post_consts, tm_post))

        def tail(x, g, o_nsa, o_fox, o_mla, mem_kv, Bg, Tg, tm):
            x = _merge(x, o_nsa, o_fox, o_mla, g, wb_[l], wo_[l], tm)
            x = _mem_attn(x, mem_kv, r2(norm_mem[l]), wqm[l], r2(mem_q_norm[l]), wom[l], Bg, Tg, _row_tile(Tg, 256))
            return _mlp(x, r2(norm_mlp[l]), wu[l], wd[l], tm, min(1024, wu.shape[2]))

        mem_p = _mem_kv_rows(mem_prompt.reshape(B * n_mem, D), r2(mem_in_norm[l]), wkvm[l], r2(mem_k_norm[l]),
                             _row_tile(B * n_mem, 512)).reshape(B, n_mem, 2 * MEM_W)
        g, nq, nqr, fq, mq, nsa_rows, win_rows, fox_rows, mla_rows, small = mixers_in(xp, tm_p, tab_p, T, tm_post_p)
        kcv = _nsa_compress(nsa_rows, B, T, *cmp_consts)
        o_nsa = _nsa_prompt(nq, nqr, small, nsa_rows, win_rows, kcv, selw_p, et_p, B, T, tq)
        o_fox = _flash_prompt(fq, fox_rows, None, _fox_cumsum(small, B, T, tkf), B, T, tq, tkf, scale=1.0)
        kk, vv = _mla_kv(mla_rows, B, T, wuk[l], wuv[l], r2(gkn[l]), r2(mla_k_norm[l, MLA_NOPE:]), g64, tkf)
        o_mla = _flash_prompt(mq, kk, vv, None, B, T, tq, tkf, scale=MLA_QK ** -0.5)
        xp = tail(xp, g, o_nsa, o_fox, o_mla, mem_p, B, T, tm_p)
        wkeep = min(WINDOW, T)
        for i, a in enumerate((nsa_rows.reshape(B, T, -1), fox_rows.reshape(B, T, -1),
                               small[:, S_LOGF:S_LOGF + N_HEADS].reshape(B, T, -1), mla_rows.reshape(B, T, -1),
                               win_rows.reshape(B, T, -1)[:, T - wkeep:], mem_p)):
            outs[i].append(a)

        g, nq, nqr, fq, mq, nsa_rows, win_rows, fox_rows, mla_rows, small = mixers_in(xs, tm_s, tab_s, Ts, tm_post_s)
        o_nsa, new_win = _nsa_sample(page_table, nq, nqr, small, nsa_rows, win_rows, state_nsa_win, *cmp_consts,
                                     selw_s, et_s, cache_nsa, l, Bs, Ts)
        o_fox = _fox_sample(page_table, fq, fox_rows, small, lstrict, cache_fox_kv, cache_lf_t, l, Bs, Ts)
        o_mla = _mla_sample(page_table, mq, mla_rows, wukt[l], wuvh[l], r2(mla_k_norm[l, :MLA_NOPE]),
                            r2(mla_k_norm[l, MLA_NOPE:]), cache_mla_t, l, Bs, Ts)
        xs = tail(xs, g, o_nsa, o_fox, o_mla, cache_mem[l], Bs, Ts, tm_s)
        for i, a in enumerate((nsa_rows.reshape(Bs, Ts, -1), fox_rows.reshape(Bs, Ts, -1),
                               small[:, S_LOGF:S_LOGF + N_HEADS].reshape(Bs, Ts, -1), mla_rows.reshape(Bs, Ts, -1),
                               new_win)):
            outs[6 + i].append(a)

    return (xp.reshape(B, T, D), xs.reshape(Bs, Ts, D)) + tuple(jnp.stack(o) for o in outs)
```

```python
import functools
import math

import jax
import jax.numpy as jnp
import numpy as np
from jax import lax
from jax.experimental import pallas as pl
from jax.experimental.pallas import tpu as pltpu

F32 = jnp.float32
MXU_DTYPE = jnp.bfloat16

HEAD_DIM = 64
N_HEADS = 8
MLA_NOPE = 64
MLA_ROPE = 32
MLA_QK = MLA_NOPE + MLA_ROPE
KV_LORA = 256
N_BRANCH = 3
L_CMP = 32
STRIDE_CMP = 16
L_SEL = 64
N_SELECT = 16
WINDOW = 512
N_MEM_HEADS = 4
MEM_HD = 128
MEM_W = N_MEM_HEADS * MEM_HD
ROPE_THETA = 10000.0
EPS = 1e-6
NEG = -1e30
HEAD_GROUPS = 2

P_QN, P_KVN, P_QF, P_KVF, P_QMN, P_QMR, P_CKV, P_SMALL, P_W = 0, 512, 896, 1408, 1536, 2048, 2304, 2560, 2688
S_GATE, S_LOGF, S_KR = 0, 24, 32

VMEM_LIMIT = 56 * 1024 * 1024


def _cp(*sem):
    return pltpu.CompilerParams(dimension_semantics=sem, vmem_limit_bytes=VMEM_LIMIT)


def _mm(a, b):
    return jnp.dot(a.astype(MXU_DTYPE), b.astype(MXU_DTYPE), preferred_element_type=F32)


def _mm_nt(a, b):
    return lax.dot_general(a.astype(MXU_DTYPE), b.astype(MXU_DTYPE), (((1,), (1,)), ((), ())),
                           preferred_element_type=F32)


def _split3(x):
    hi = x.astype(MXU_DTYPE)
    r = x - hi.astype(F32)
    mid = r.astype(MXU_DTYPE)
    lo = (r - mid.astype(F32)).astype(MXU_DTYPE)
    return hi, mid, lo


def _mm3(a, b_exact):
    b = b_exact.astype(MXU_DTYPE)
    return sum(jnp.dot(p, b, preferred_element_type=F32) for p in _split3(a))


def _mm3_nt_lhs_exact(a_exact, b):
    a = a_exact.astype(MXU_DTYPE)
    dn = (((1,), (1,)), ((), ()))
    return sum(lax.dot_general(a, p, dn, preferred_element_type=F32) for p in _split3(b))


def _rms(x, g):
    return x * lax.rsqrt(jnp.mean(x * x, axis=-1, keepdims=True) + EPS) * g


def _mm3_lhs_exact(a_exact, b):
    a = a_exact.astype(MXU_DTYPE)
    return sum(jnp.dot(a, p, preferred_element_type=F32) for p in _split3(b))


def _softmax_update(s, mask, m_old, l_old, acc_old, v):
    if mask is not None:
        s = jnp.where(mask, s, NEG)
    m_new = jnp.maximum(m_old, jnp.max(s, axis=-1, keepdims=True))
    alpha = jnp.exp(m_old - m_new)
    p = jnp.exp(s - m_new)
    if mask is not None:
        p = jnp.where(mask, p, 0.0)
    l_new = alpha * l_old + jnp.sum(p, axis=-1, keepdims=True)
    acc_new = alpha * acc_old + (v(p) if callable(v) else _mm(p, v))
    return m_new, l_new, acc_new


def _masked_softmax(s, mask):
    s = jnp.where(mask, s, NEG)
    p = jnp.where(mask, jnp.exp(s - jnp.max(s, axis=-1, keepdims=True)), 0.0)
    return p / jnp.maximum(jnp.sum(p, axis=-1, keepdims=True), 1e-30)


def _stack_rows(x3):
    return x3.reshape(x3.shape[0] * x3.shape[1], x3.shape[2])


def _topk_mask(score, n_valid, k_top):
    rows, n = score.shape
    rpad = -(-rows // 128) * 128
    if rpad != rows:
        score = jnp.concatenate([score, jnp.full((rpad - rows, n), -jnp.inf, F32)], axis=0)
    nv = -(-n_valid // 8) * 8
    st = score.T[:nv]
    ids = lax.broadcasted_iota(jnp.int32, st.shape, 0)
    sel = jnp.zeros(st.shape, F32)
    for _ in range(k_top):
        mx = jnp.max(st, axis=0, keepdims=True)
        idx = jnp.min(jnp.where(st == mx, ids, 1 << 30), axis=0, keepdims=True)
        pick = ids == idx
        sel = jnp.where(pick & (mx > -jnp.inf), 1.0, sel)
        st = jnp.where(pick, -jnp.inf, st)
    if nv != n:
        sel = jnp.concatenate([sel, jnp.zeros((n - nv, rpad), F32)], axis=0)
    return sel.T[:rows]


def _rope_tab_body(pos_ref, inv_ref, o_ref):
    ang = pos_ref[...] * inv_ref[...]
    c = jnp.cos(ang)
    s = jnp.sin(ang)
    lane = lax.broadcasted_iota(jnp.int32, (ang.shape[0], 128), 1)
    first32 = (lane % 64) < 32
    first16 = (lane % 32) < 16
    c32, s32 = c[:, :128], s[:, :128]
    c16, s16 = c[:, 128:], s[:, 128:]
    o_ref[:, 0:128] = c32
    o_ref[:, 128:256] = jnp.where(first32, -s32, 0.0)
    o_ref[:, 256:384] = jnp.where(first32, 0.0, s32)
    o_ref[:, 384:512] = c16
    o_ref[:, 512:640] = jnp.where(first16, -s16, 0.0)
    o_ref[:, 640:768] = jnp.where(first16, 0.0, s16)


def _rope_tables(pos):
    T = pos.shape[0]
    inv32 = ROPE_THETA ** (-jnp.arange(32, dtype=F32) / 32)
    inv16 = ROPE_THETA ** (-jnp.arange(16, dtype=F32) / 16)
    inv = jnp.concatenate([jnp.tile(inv32, 4), jnp.tile(inv16, 8)])[None, :]
    return pl.pallas_call(
        _rope_tab_body,
        name="rope_tables",
        out_shape=jax.ShapeDtypeStruct((T, 768), F32),
    )(pos[:, None], inv)


def _norm_mm_body(x_ref, g_ref, w_ref, o_ref):
    o_ref[...] = _mm(_rms(x_ref[...], g_ref[...]), w_ref[...])


def _norm_matmul(x, g, w, tm):
    M, K = x.shape
    N = w.shape[1]
    return pl.pallas_call(
        _norm_mm_body,
        name="norm_matmul",
        grid=(M // tm,),
        in_specs=[pl.BlockSpec((tm, K), lambda i: (i, 0)),
                  pl.BlockSpec((1, K), lambda i: (0, 0)),
                  pl.BlockSpec((K, N), lambda i: (0, 0))],
        out_specs=pl.BlockSpec((tm, N), lambda i: (i, 0)),
        out_shape=jax.ShapeDtypeStruct((M, N), F32),
        compiler_params=_cp("parallel"),
    )(x, g, w)


def _mem_kv_body(x_ref, g_ref, w_ref, gk_ref, o_ref):
    kv = _mm(_rms(x_ref[...], g_ref[...]), w_ref[...])
    for h in range(N_MEM_HEADS):
        sl = slice(h * MEM_HD, (h + 1) * MEM_HD)
        o_ref[:, sl] = _rms(kv[:, sl], gk_ref[...])
    o_ref[:, MEM_W:] = kv[:, MEM_W:]


def _mem_kv_rows(mem2d, g_in, w_kv, g_k, tm):
    M, K = mem2d.shape
    N = w_kv.shape[1]
    return pl.pallas_call(
        _mem_kv_body,
        name="mem_kv_rows",
        grid=(M // tm,),
        in_specs=[pl.BlockSpec((tm, K), lambda i: (i, 0)),
                  pl.BlockSpec((1, K), lambda i: (0, 0)),
                  pl.BlockSpec((K, N), lambda i: (0, 0)),
                  pl.BlockSpec((1, MEM_HD), lambda i: (0, 0))],
        out_specs=pl.BlockSpec((tm, N), lambda i: (i, 0)),
        out_shape=jax.ShapeDtypeStruct((M, N), F32),
        compiler_params=_cp("parallel"),
    )(mem2d, g_in, w_kv, g_k)


def _post_body(p_ref, tab_ref, gqn_ref, gkv_ref, mkv_ref, gqf_ref, gfk_ref, gmn_ref, gmr_ref, gckv_ref,
               bfox_ref, g64_ref, gx_ref, gxt_ref, g32_ref,
               nq_ref, nqr_ref, fq_ref, mq_ref, nsa_ref, win_ref, fox_ref, mla_ref, small_ref):
    tab = tab_ref[...]
    c32, sp32, sm32, c16, sp16, sm16 = [tab[:, i * 128:(i + 1) * 128] for i in range(6)]

    def tile(t, n):
        return t if n == 1 else jnp.concatenate([t] * n, axis=1)

    def rope(x, c, sp, sm, s):
        w = x.shape[1]
        n = w // 128
        return x * tile(c, n) + pltpu.roll(x, w - s, 1) * tile(sp, n) + pltpu.roll(x, s, 1) * tile(sm, n)

    g64 = g64_ref[...]
    scale = HEAD_DIM ** -0.5

    x = p_ref[:, P_QN:P_QN + 512]
    y = x * lax.rsqrt(_mm3(x * x, g64) / HEAD_DIM + EPS) * gqn_ref[...]
    yr = rope(y, c32, sp32, sm32, 32)
    xf = p_ref[:, P_QF:P_QF + 512]
    yf = xf * lax.rsqrt(_mm3(xf * xf, g64) / HEAD_DIM + EPS) * gqf_ref[...]
    for h in range(N_HEADS):
        sl = slice(h * HEAD_DIM, (h + 1) * HEAD_DIM)
        nq_ref[h] = (y[:, sl] * scale).astype(nq_ref.dtype)
        nqr_ref[h] = (yr[:, sl] * scale).astype(nqr_ref.dtype)
        fq_ref[h] = (yf[:, sl] * scale).astype(fq_ref.dtype)

    x = p_ref[:, P_KVN:P_KVN + 384]
    normed = mkv_ref[...] > 0.5
    yn = x * lax.rsqrt(_mm3(x * x, g64[:384, :384]) / HEAD_DIM + EPS) * gkv_ref[...]
    y = jnp.where(normed, rope(yn, c32, sp32, sm32, 32), x)
    nsa_ref[...] = y[:, :256]
    win_ref[...] = y[:, 256:]

    x = p_ref[:, P_KVF:P_KVF + 128]
    lane = lax.broadcasted_iota(jnp.int32, x.shape, 1)
    yn = x * lax.rsqrt(_mm3(x * x, g64[:128, :128]) / HEAD_DIM + EPS) * gfk_ref[...]
    fox_ref[...] = jnp.where(lane < HEAD_DIM, yn, x)

    xn = p_ref[:, P_QMN:P_QMN + 512]
    xr = rope(p_ref[:, P_QMR:P_QMR + 256], c16, sp16, sm16, 16)
    xn2, xr2 = xn * xn, xr * xr
    ms_n = (_mm3(xn2, g64) + _mm3(xr2, gx_ref[...])) / MLA_QK
    ms_r = (_mm3(xn2, gxt_ref[...]) + _mm3(xr2, g32_ref[...])) / MLA_QK
    yn = xn * lax.rsqrt(ms_n + EPS) * gmn_ref[...]
    yr = xr * lax.rsqrt(ms_r + EPS) * gmr_ref[...]
    for h in range(N_HEADS):
        mq_ref[h, :, 0:MLA_NOPE] = yn[:, h * MLA_NOPE:(h + 1) * MLA_NOPE].astype(mq_ref.dtype)
        mq_ref[h, :, MLA_NOPE:MLA_QK] = yr[:, h * MLA_ROPE:(h + 1) * MLA_ROPE].astype(mq_ref.dtype)

    x = p_ref[:, P_SMALL:P_SMALL + 128]
    sig = jax.nn.sigmoid(x)
    lf = jax.nn.log_sigmoid(x + bfox_ref[...])
    rr = rope(x, c16, sp16, sm16, 16)
    small = jnp.where(lane < S_LOGF, sig, jnp.where(lane < S_KR, lf, jnp.where(lane < S_KR + MLA_ROPE, rr, 0.0)))
    small_ref[...] = small

    x = p_ref[:, P_CKV:P_CKV + KV_LORA]
    mla_ref[:, 0:KV_LORA] = _rms(x, gckv_ref[...])
    mla_ref[:, KV_LORA:KV_LORA + MLA_ROPE] = small[:, S_KR:S_KR + MLA_ROPE]


def _post(p, tab, T, consts, tm):
    M = p.shape[0]
    if T >= tm:
        nt = T // tm
        tab_map = lambda i: (i % nt, 0)
    else:
        tab_map = lambda i: (0, 0)
    row = lambda w: pl.BlockSpec((tm, w), lambda i: (i, 0))
    head = lambda d: pl.BlockSpec((N_HEADS, tm, d), lambda i: (0, i, 0))
    full = lambda a: pl.BlockSpec(a.shape, lambda i: (0,) * a.ndim)
    qd = MXU_DTYPE
    out_shape = [jax.ShapeDtypeStruct((N_HEADS, M, HEAD_DIM), qd)] * 3 + [
        jax.ShapeDtypeStruct((N_HEADS, M, MLA_QK), qd),
        jax.ShapeDtypeStruct((M, 256), F32), jax.ShapeDtypeStruct((M, 128), F32),
        jax.ShapeDtypeStruct((M, 128), F32), jax.ShapeDtypeStruct((M, KV_LORA + MLA_ROPE), F32),
        jax.ShapeDtypeStruct((M, 128), F32)]
    out_specs = [head(HEAD_DIM)] * 3 + [head(MLA_QK), row(256), row(128), row(128), row(KV_LORA + MLA_ROPE), row(128)]
    return pl.pallas_call(
        _post_body,
        name="post",
        grid=(M // tm,),
        in_specs=[row(P_W), pl.BlockSpec((tm, 768), tab_map)] + [full(c) for c in consts],
        out_specs=out_specs,
        out_shape=out_shape,
        compiler_params=_cp("parallel"),
    )(p, tab, *consts)


def _compress(load_xl, nch, pe_ref, w_ref, g_ref):
    acc_a = jnp.zeros((nch, 128), F32)
    acc_b = jnp.zeros((nch, 128), F32)
    for l in range(STRIDE_CMP):
        xl = load_xl(l)
        acc_a = acc_a + _mm(xl + pe_ref[l:l + 1, :], w_ref[l])
        acc_b = acc_b + _mm(xl + pe_ref[STRIDE_CMP + l:STRIDE_CMP + l + 1, :], w_ref[STRIDE_CMP + l])
    kcv = acc_a + pltpu.roll(acc_b, nch - 1, 0)
    lane = lax.broadcasted_iota(jnp.int32, kcv.shape, 1)
    is_k = lane < HEAD_DIM
    ms = jnp.sum(jnp.where(is_k, kcv * kcv, 0.0), axis=-1, keepdims=True) / HEAD_DIM
    return jnp.where(is_k, kcv * lax.rsqrt(ms + EPS) * g_ref[...], kcv)


def _cmp_body(rows_ref, pe_ref, w_ref, g_ref, o_ref, *, nch):
    o_ref[...] = _compress(lambda l: rows_ref[pl.ds(l, nch, stride=STRIDE_CMP), :], nch, pe_ref, w_ref, g_ref)


def _nsa_compress(nsa_rows, B, S, pe, wblk, gk):
    nch = S // STRIDE_CMP
    return pl.pallas_call(
        functools.partial(_cmp_body, nch=nch),
        name="nsa_compress",
        grid=(B,),
        in_specs=[pl.BlockSpec((S, 128), lambda b: (b, 0)),
                  pl.BlockSpec(pe.shape, lambda b: (0, 0)),
                  pl.BlockSpec(wblk.shape, lambda b: (0, 0, 0)),
                  pl.BlockSpec((1, 128), lambda b: (0, 0))],
        out_specs=pl.BlockSpec((None, nch, 128), lambda b: (b, 0, 0)),
        out_shape=jax.ShapeDtypeStruct((B, nch, 128), F32),
        compiler_params=_cp("parallel"),
    )(nsa_rows, pe, wblk, gk)


def _nsa_prompt_body(nq_ref, nqr_ref, small_ref, rows_ref, win_ref, kcv_ref, selw_ref, et_ref, o_ref,
                     m_ref, l_ref, acc_ref, *, tq, tk, wlen, k_top):
    H = N_HEADS
    G = HEAD_GROUPS
    Hg = H // G
    groups = [slice(g * Hg, (g + 1) * Hg) for g in range(G)]
    T = rows_ref.shape[0]
    qi = pl.program_id(1)
    q0 = qi * tq
    qpos = q0 + lax.broadcasted_iota(jnp.int32, (tq, 1), 0)
    nch = kcv_ref.shape[0]
    nsp = selw_ref.shape[1]

    def gate(j, hs):
        return jnp.concatenate([small_ref[:, S_GATE + 3 * h + j:S_GATE + 3 * h + j + 1]
                                for h in range(hs.start, hs.stop)], axis=0).reshape(Hg, tq, 1)

    kc = kcv_ref[:, 0:HEAD_DIM]
    vc = kcv_ref[:, HEAD_DIM:]
    cmp_end = lax.broadcasted_iota(jnp.int32, (1, nch), 1) * STRIDE_CMP + (L_CMP - 1)
    mask_c = (cmp_end <= qpos)[None]
    out = []
    psum = jnp.zeros((tq, nch), F32)
    for hs in groups:
        p = _masked_softmax(_mm_nt(_stack_rows(nq_ref[hs]), kc).reshape(Hg, tq, nch), mask_c)
        out.append(gate(0, hs) * _mm(_stack_rows(p), vc).reshape(Hg, tq, HEAD_DIM))
        psum = psum + jnp.sum(p, axis=0)
    imp = _mm3(psum, selw_ref[...])

    ids = lax.broadcasted_iota(jnp.int32, (1, nsp), 1)
    cur = qpos // L_SEL
    forced = (ids == 0) | (ids == cur) | (ids == cur - 1)
    avail = ids * L_SEL <= qpos
    score = jnp.where(forced, jnp.inf, jnp.where(avail, imp, -jnp.inf))
    selm = _topk_mask(score, -(-T // L_SEL), k_top).astype(MXU_DTYPE)

    m_ref[...] = jnp.full(m_ref.shape, NEG, F32)
    l_ref[...] = jnp.zeros(l_ref.shape, F32)
    acc_ref[...] = jnp.zeros(acc_ref.shape, F32)

    def sel_step(kt, carry):
        k0 = pl.multiple_of(kt * tk, tk)
        blk = _mm_nt(selm, et_ref[pl.ds(k0, tk), :]) > 0.5
        kpos = k0 + lax.broadcasted_iota(jnp.int32, (1, tk), 1)
        mask = (blk & (kpos <= qpos))[None]
        kk = rows_ref[pl.ds(k0, tk), 128:192]
        vv = rows_ref[pl.ds(k0, tk), 192:256]
        for hs in groups:
            s3 = _mm_nt(_stack_rows(nqr_ref[hs]), kk).reshape(Hg, tq, tk)
            m, l, a = _softmax_update(s3, mask, m_ref[hs], l_ref[hs], acc_ref[hs],
                                      lambda w: _mm(_stack_rows(w), vv).reshape(Hg, tq, HEAD_DIM))
            m_ref[hs] = m
            l_ref[hs] = l
            acc_ref[hs] = a
        return carry

    lax.fori_loop(0, (q0 + tq - 1) // tk + 1, sel_step, 0)

    w0 = pl.multiple_of(jnp.clip(q0 - WINDOW, 0, T - wlen), 8)
    kpos = w0 + lax.broadcasted_iota(jnp.int32, (1, wlen), 1)
    mask_w = ((kpos > qpos - WINDOW) & (kpos <= qpos))[None]
    kw = win_ref[pl.ds(w0, wlen), 0:HEAD_DIM]
    vw = win_ref[pl.ds(w0, wlen), HEAD_DIM:]
    for g, hs in enumerate(groups):
        o = out[g] + gate(1, hs) * (acc_ref[hs] / jnp.maximum(l_ref[hs], 1e-30))
        p = _masked_softmax(_mm_nt(_stack_rows(nqr_ref[hs]), kw).reshape(Hg, tq, wlen), mask_w)
        o = o + gate(2, hs) * _mm(_stack_rows(p), vw).reshape(Hg, tq, HEAD_DIM)
        for i in range(Hg):
            h = hs.start + i
            o_ref[:, h * HEAD_DIM:(h + 1) * HEAD_DIM] = o[i]


def _nsa_prompt(nq, nqr, small, nsa_rows, win_rows, kcv, selw, et, B, T, tq):
    nqt = T // tq
    tk = _key_tile(T)
    wlen = min(T, WINDOW + tq)
    assert (T - wlen) % 8 == 0 and T % tk == 0
    n_sel = -(-T // L_SEL)
    k_top = min(N_SELECT, n_sel)
    nch = kcv.shape[1]
    head = pl.BlockSpec((N_HEADS, tq, HEAD_DIM), lambda b, i: (0, b * nqt + i, 0))
    return pl.pallas_call(
        functools.partial(_nsa_prompt_body, tq=tq, tk=tk, wlen=wlen, k_top=k_top),
        grid=(B, nqt),
        in_specs=[head, head,
                  pl.BlockSpec((tq, 128), lambda b, i: (b * nqt + i, 0)),
                  pl.BlockSpec((T, 256), lambda b, i: (b, 0)),
                  pl.BlockSpec((T, 128), lambda b, i: (b, 0)),
                  pl.BlockSpec((None, nch, 128), lambda b, i: (b, 0, 0)),
                  pl.BlockSpec(selw.shape, lambda b, i: (0, 0)),
                  pl.BlockSpec(et.shape, lambda b, i: (0, 0))],
        out_specs=pl.BlockSpec((tq, N_HEADS * HEAD_DIM), lambda b, i: (b * nqt + i, 0)),
        out_shape=jax.ShapeDtypeStruct((B * T, N_HEADS * HEAD_DIM), F32),
        scratch_shapes=[pltpu.VMEM((N_HEADS, tq, 1), F32), pltpu.VMEM((N_HEADS, tq, 1), F32),
                        pltpu.VMEM((N_HEADS, tq, HEAD_DIM), F32)],
        compiler_params=_cp("parallel", "arbitrary"),
        name="nsa_prompt",
    )(nq, nqr, small, nsa_rows, win_rows, kcv, selw, et)


def _cumsum_body(small_ref, cst_ref, *, S, tc):
    r = lax.broadcasted_iota(jnp.int32, (tc, tc), 0)
    c = lax.broadcasted_iota(jnp.int32, (tc, tc), 1)
    upper = (r <= c).astype(MXU_DTYPE)

    def step(i, carry):
        k0 = pl.multiple_of(i * tc, tc)
        xt = small_ref[pl.ds(k0, tc), :].T
        ct = _mm3(xt, upper) + carry
        cst_ref[i] = ct[S_LOGF:S_LOGF + N_HEADS, :]
        return ct[:, tc - 1:tc]

    lax.fori_loop(0, S // tc, step, jnp.zeros((128, 1), F32))


def _fox_cumsum(small, B, S, tc):
    return pl.pallas_call(
        functools.partial(_cumsum_body, S=S, tc=tc),
        name="forget_cumsum",
        grid=(B,),
        in_specs=[pl.BlockSpec((S, 128), lambda b: (b, 0))],
        out_specs=pl.BlockSpec((None, S // tc, N_HEADS, tc), lambda b: (b, 0, 0, 0)),
        out_shape=jax.ShapeDtypeStruct((B, S // tc, N_HEADS, tc), F32),
        compiler_params=_cp("parallel"),
    )(small)


def _flash_body(*refs, tq, tk, dv, per_head, decay, scale):
    if decay:
        q_ref, k_ref, c_ref, o_ref, m_ref, l_ref, acc_ref = refs
        v_ref = None
    elif per_head:
        q_ref, k_ref, v_ref, o_ref, m_ref, l_ref, acc_ref = refs
    else:
        q_ref, k_ref, o_ref, m_ref, l_ref, acc_ref = refs
        v_ref = None
    H = N_HEADS
    qi = pl.program_id(1)
    kj = pl.program_id(2)
    q0 = qi * tq
    k0 = kj * tk

    @pl.when(kj == 0)
    def _():
        m_ref[...] = jnp.full(m_ref.shape, NEG, F32)
        l_ref[...] = jnp.zeros(l_ref.shape, F32)
        acc_ref[...] = jnp.zeros(acc_ref.shape, F32)

    @pl.when(k0 <= q0 + tq - 1)
    def _():
        qpos = q0 + lax.broadcasted_iota(jnp.int32, (tq, 1), 0)
        kpos = k0 + lax.broadcasted_iota(jnp.int32, (1, tk), 1)
        mask = (kpos <= qpos)[None]
        Hg = H // HEAD_GROUPS
        for g in range(HEAD_GROUPS):
            hs = slice(g * Hg, (g + 1) * Hg)
            q = q_ref[hs]
            if per_head:
                s3 = lax.dot_general(q, k_ref[hs], (((2,), (2,)), ((0,), (0,))), preferred_element_type=F32)
                pv = lambda w, hs=hs: lax.dot_general(w.astype(MXU_DTYPE), v_ref[hs], (((2,), (1,)), ((0,), (0,))),
                                               preferred_element_type=F32)
            else:
                vv = k_ref[:, HEAD_DIM:2 * HEAD_DIM]
                s3 = _mm_nt(_stack_rows(q), k_ref[:, 0:HEAD_DIM]).reshape(Hg, tq, tk)
                pv = lambda w, vv=vv: _mm(_stack_rows(w), vv).reshape(Hg, tq, dv)
            if scale != 1.0:
                s3 = s3 * scale
            if decay:
                s3 = s3 - c_ref[hs][:, None, :]
            m, l, a = _softmax_update(s3, mask, m_ref[hs], l_ref[hs], acc_ref[hs], pv)
            m_ref[hs] = m
            l_ref[hs] = l
            acc_ref[hs] = a

    @pl.when(kj == pl.num_programs(2) - 1)
    def _():
        o = acc_ref[...] / jnp.maximum(l_ref[...], 1e-30)
        for h in range(H):
            o_ref[:, h * dv:(h + 1) * dv] = o[h]


def _flash_prompt(q, k, v, cst, B, T, tq, tk, *, scale):
    nqt, nkt = T // tq, T // tk
    d = q.shape[-1]
    per_head = v is not None
    decay = cst is not None
    last = lambda i: (i * tq + tq - 1) // tk
    in_specs = [pl.BlockSpec((N_HEADS, tq, d), lambda b, i, j: (0, b * nqt + i, 0))]
    args = [q]
    if per_head:
        dv = v.shape[-1]
        in_specs += [pl.BlockSpec((None, N_HEADS, tk, d), lambda b, i, j: (b, 0, jnp.minimum(j, last(i)), 0)),
                     pl.BlockSpec((None, N_HEADS, tk, dv), lambda b, i, j: (b, 0, jnp.minimum(j, last(i)), 0))]
        args += [k, v]
    else:
        dv = HEAD_DIM
        in_specs += [pl.BlockSpec((tk, 128), lambda b, i, j: (b * nkt + jnp.minimum(j, last(i)), 0))]
        args += [k]
    if decay:
        assert cst.shape[-1] == tk
        in_specs += [pl.BlockSpec((None, None, N_HEADS, tk), lambda b, i, j: (b, jnp.minimum(j, last(i)), 0, 0))]
        args += [cst]
    return pl.pallas_call(
        functools.partial(_flash_body, tq=tq, tk=tk, dv=dv, per_head=per_head, decay=decay, scale=scale),
        grid=(B, nqt, nkt),
        in_specs=in_specs,
        out_specs=pl.BlockSpec((tq, N_HEADS * dv), lambda b, i, j: (b * nqt + i, 0)),
        out_shape=jax.ShapeDtypeStruct((B * T, N_HEADS * dv), F32),
        scratch_shapes=[pltpu.VMEM((N_HEADS, tq, 1), F32), pltpu.VMEM((N_HEADS, tq, 1), F32),
                        pltpu.VMEM((N_HEADS, tq, dv), F32)],
        compiler_params=_cp("parallel", "parallel", "arbitrary"),
        name="flash_latent" if per_head else "flash_forget",
    )(*args)


def _mla_kv_body(r_ref, wuk_ref, wuv_ref, gkn_ref, gkr_ref, g64_ref, k_ref, v_ref):
    ckv = r_ref[:, 0:KV_LORA]
    kr = r_ref[:, KV_LORA:KV_LORA + MLA_ROPE]
    kn = _mm(ckv, wuk_ref[...])
    ss = _mm3(kn * kn, g64_ref[...]) + jnp.sum(kr * kr, axis=-1, keepdims=True)
    inv = lax.rsqrt(ss / MLA_QK + EPS)
    knn = kn * gkn_ref[...] * inv
    krg = kr * gkr_ref[...]
    vv = _mm(ckv, wuv_ref[...])
    for h in range(N_HEADS):
        sl = slice(h * HEAD_DIM, (h + 1) * HEAD_DIM)
        k_ref[h, :, 0:MLA_NOPE] = knn[:, sl].astype(k_ref.dtype)
        k_ref[h, :, MLA_NOPE:MLA_QK] = (krg * inv[:, h * HEAD_DIM:h * HEAD_DIM + MLA_ROPE]).astype(k_ref.dtype)
        v_ref[h] = vv[:, sl].astype(v_ref.dtype)


def _mla_kv(mla_rows, B, T, wuk, wuv, gkn, gkr, g64, tk):
    nkt = T // tk
    full = lambda a: pl.BlockSpec(a.shape, lambda b, j: (0,) * a.ndim)
    return pl.pallas_call(
        _mla_kv_body,
        name="latent_kv",
        grid=(B, nkt),
        in_specs=[pl.BlockSpec((tk, KV_LORA + MLA_ROPE), lambda b, j: (b * nkt + j, 0)),
                  full(wuk), full(wuv), full(gkn), full(gkr), full(g64)],
        out_specs=[pl.BlockSpec((None, N_HEADS, tk, MLA_QK), lambda b, j: (b, 0, j, 0)),
                   pl.BlockSpec((None, N_HEADS, tk, HEAD_DIM), lambda b, j: (b, 0, j, 0))],
        out_shape=[jax.ShapeDtypeStruct((B, N_HEADS, T, MLA_QK), MXU_DTYPE),
                   jax.ShapeDtypeStruct((B, N_HEADS, T, HEAD_DIM), MXU_DTYPE)],
        compiler_params=_cp("parallel", "parallel"),
    )(mla_rows, wuk, wuv, gkn, gkr, g64)


def _merge_body(x_ref, o1_ref, o2_ref, o3_ref, g_ref, wb_ref, wo_ref, out_ref):
    d = x_ref.shape[1]
    acc = jnp.zeros(x_ref.shape, F32)
    for b, o_ref in enumerate((o1_ref, o2_ref, o3_ref)):
        acc = acc + jax.nn.sigmoid(g_ref[:, b * d:(b + 1) * d]) * _mm(o_ref[...], wb_ref[b])
    out_ref[...] = x_ref[...] + _mm(acc, wo_ref[...])


def _merge(x, o1, o2, o3, g, wb, wo, tm):
    M, D = x.shape
    row = lambda w: pl.BlockSpec((tm, w), lambda i: (i, 0))
    return pl.pallas_call(
        _merge_body,
        name="merge",
        grid=(M // tm,),
        in_specs=[row(D), row(o1.shape[1]), row(o2.shape[1]), row(o3.shape[1]), row(g.shape[1]),
                  pl.BlockSpec(wb.shape, lambda i: (0, 0, 0)), pl.BlockSpec(wo.shape, lambda i: (0, 0))],
        out_specs=row(D),
        out_shape=jax.ShapeDtypeStruct((M, D), F32),
        compiler_params=_cp("parallel"),
    )(x, o1, o2, o3, g, wb, wo)


def _mem_attn_body(x_ref, kv_ref, g_ref, wq_ref, gq_ref, wo_ref, o_ref):
    bb = kv_ref.shape[0]
    x = x_ref[...]
    rows = x.shape[0] // bb
    q = _mm(_rms(x, g_ref[...]), wq_ref[...])
    per_batch = []
    for i in range(bb):
        outs = []
        for h in range(N_MEM_HEADS):
            sl = slice(h * MEM_HD, (h + 1) * MEM_HD)
            qh = _rms(q[i * rows:(i + 1) * rows, sl], gq_ref[...])
            s = _mm_nt(qh, kv_ref[i, :, sl]) * MEM_HD ** -0.5
            p = jnp.exp(s - jnp.max(s, axis=-1, keepdims=True))
            p = p / jnp.sum(p, axis=-1, keepdims=True)
            outs.append(_mm(p, kv_ref[i, :, MEM_W + h * MEM_HD:MEM_W + (h + 1) * MEM_HD]))
        per_batch.append(jnp.concatenate(outs, axis=1))
    om = per_batch[0] if bb == 1 else jnp.concatenate(per_batch, axis=0)
    o_ref[...] = x + _mm(om, wo_ref[...])


def _mem_attn(x, mem_kv, g, wq, gq, wo, B, T, tm):
    D = x.shape[1]
    nt = T // tm
    n_mem = mem_kv.shape[1]
    bb = next(b for b in (8, 4, 2, 1) if B % b == 0) if nt == 1 and T <= 64 else 1
    full = lambda a: pl.BlockSpec(a.shape, lambda b, i: (0,) * a.ndim)
    return pl.pallas_call(
        _mem_attn_body,
        name="mem_attn",
        grid=(B // bb, nt),
        in_specs=[pl.BlockSpec((bb * tm, D), lambda b, i: (b * nt + i, 0)),
                  pl.BlockSpec((bb, n_mem, 2 * MEM_W), lambda b, i: (b, 0, 0)),
                  full(g), full(wq), full(gq), full(wo)],
        out_specs=pl.BlockSpec((bb * tm, D), lambda b, i: (b * nt + i, 0)),
        out_shape=jax.ShapeDtypeStruct(x.shape, F32),
        compiler_params=_cp("parallel", "parallel"),
    )(x, mem_kv, g, wq, gq, wo)


def _mlp_body(x_ref, g_ref, wu_ref, wd_ref, o_ref, h_ref, acc_ref):
    j = pl.program_id(1)

    @pl.when(j == 0)
    def _():
        h_ref[...] = _rms(x_ref[...], g_ref[...]).astype(h_ref.dtype)
        acc_ref[...] = jnp.zeros(acc_ref.shape, F32)

    u = jnp.dot(h_ref[...], wu_ref[...], preferred_element_type=F32)
    acc_ref[...] += _mm(jnp.square(jnp.maximum(u, 0.0)), wd_ref[...])

    @pl.when(j == pl.num_programs(1) - 1)
    def _():
        o_ref[...] = x_ref[...] + acc_ref[...]


def _mlp(x, g, wu, wd, tm, tf):
    M, D = x.shape
    FF = wu.shape[1]
    return pl.pallas_call(
        _mlp_body,
        name="mlp",
        grid=(M // tm, FF // tf),
        in_specs=[pl.BlockSpec((tm, D), lambda i, j: (i, 0)),
                  pl.BlockSpec((1, D), lambda i, j: (0, 0)),
                  pl.BlockSpec((D, tf), lambda i, j: (0, j)),
                  pl.BlockSpec((tf, D), lambda i, j: (j, 0))],
        out_specs=pl.BlockSpec((tm, D), lambda i, j: (i, 0)),
        out_shape=jax.ShapeDtypeStruct((M, D), F32),
        scratch_shapes=[pltpu.VMEM((tm, D), MXU_DTYPE), pltpu.VMEM((tm, D), F32)],
        compiler_params=_cp("parallel", "arbitrary"),
    )(x, g, wu, wd)


def _page_copies(pt_ref, bb, slot, n_pages, page, pairs, start):
    def body(p, carry):
        pg = pt_ref[bb, p]
        for src_fn, dst_fn, sem in pairs:
            cp = pltpu.make_async_copy(src_fn(pg), dst_fn(slot, p), sem.at[slot])
            if start:
                cp.start()
            else:
                cp.wait()
        return carry

    lax.fori_loop(0, n_pages, body, 0)


def _gather_pipeline(pt_ref, n_pages, page, pairs):
    b = pl.program_id(0)
    nb = pl.num_programs(0)

    @pl.when(b == 0)
    def _():
        _page_copies(pt_ref, 0, 0, n_pages, page, pairs, True)

    @pl.when(b + 1 < nb)
    def _():
        _page_copies(pt_ref, b + 1, (b + 1) % 2, n_pages, page, pairs, True)

    slot = b % 2
    _page_copies(pt_ref, b, slot, n_pages, page, pairs, False)
    return slot


def _row_mod(rows, T):
    return lax.broadcasted_iota(jnp.int32, (rows, 1), 0) % T


def _heads_to_lanes(o_ref, o, T, dv):
    for h in range(N_HEADS):
        o_ref[:, h * dv:(h + 1) * dv] = o[h * T:(h + 1) * T, :]


def _fox_sample_body(pt_ref, q_ref, newkv_ref, newsmall_ref, lstrict_ref, kv_hbm, lf_hbm, o_ref, kvbuf, lfbuf, ct_ref,
                     sem_kv, sem_lf, *, layer, n_pages, page, tk, T):
    pairs = [(lambda pg: kv_hbm.at[layer, pg], lambda s, p: kvbuf.at[s, pl.ds(p * page, page)], sem_kv),
             (lambda pg: lf_hbm.at[layer, pg], lambda s, p: lfbuf.at[s, p], sem_lf)]
    slot = _gather_pipeline(pt_ref, n_pages, page, pairs)
    H = N_HEADS
    R = H * T
    P = n_pages * page

    upper = (lax.broadcasted_iota(jnp.int32, (page, page), 0)
             <= lax.broadcasted_iota(jnp.int32, (page, page), 1)).astype(MXU_DTYPE)
    local = _mm3(lfbuf[slot].reshape(n_pages * H, page), upper)
    tot = jnp.broadcast_to(local[:, page - 1:page], local.shape)
    ct_all = local + _mm3_lhs_exact(lstrict_ref[...], tot)
    ct_ref[...] = ct_all.reshape(n_pages, H, page)
    total = ct_all[(n_pages - 1) * H:, page - 1:page]
    eye = (lax.broadcasted_iota(jnp.int32, (H, H), 0) == lax.broadcasted_iota(jnp.int32, (H, H), 1)).astype(MXU_DTYPE)
    upper_t = (lax.broadcasted_iota(jnp.int32, (T, T), 0) <= lax.broadcasted_iota(jnp.int32, (T, T), 1)).astype(MXU_DTYPE)
    ct_new = _mm3(_mm3_nt_lhs_exact(eye, newsmall_ref[:, S_LOGF:S_LOGF + H]), upper_t) + total

    q = _stack_rows(q_ref[...])
    ppt = tk // page

    def step(j, carry):
        m, l, acc = carry
        k0 = pl.multiple_of(j * tk, tk)
        kv = kvbuf[slot, pl.ds(k0, tk), :]
        ct = jnp.concatenate([ct_ref[j * ppt + i] for i in range(ppt)], axis=1)
        s = (_mm_nt(q, kv[:, 0:HEAD_DIM]).reshape(H, T, tk) - ct[:, None, :]).reshape(R, tk)
        return _softmax_update(s, None, m, l, acc, kv[:, HEAD_DIM:])

    init = (jnp.full((R, 1), NEG, F32), jnp.zeros((R, 1), F32), jnp.zeros((R, HEAD_DIM), F32))
    m, l, acc = lax.fori_loop(0, P // tk, step, init, unroll=2)

    knew = newkv_ref[...]
    s = (_mm_nt(q, knew[:, 0:HEAD_DIM]).reshape(H, T, T) - ct_new[:, None, :]).reshape(R, T)
    mask = lax.broadcasted_iota(jnp.int32, (1, T), 1) <= _row_mod(R, T)
    m, l, acc = _softmax_update(s, mask, m, l, acc, knew[:, HEAD_DIM:])
    _heads_to_lanes(o_ref, acc / jnp.maximum(l, 1e-30), T, HEAD_DIM)


def _fox_sample(page_table, fq, fox_rows, small, lstrict, cache_kv, cache_lf_t, layer, B, T):
    n_pages = page_table.shape[1]
    page = cache_kv.shape[2]
    P = n_pages * page
    tk = min(2048, P)
    grid_spec = pltpu.PrefetchScalarGridSpec(
        num_scalar_prefetch=1,
        grid=(B,),
        in_specs=[pl.BlockSpec((N_HEADS, T, HEAD_DIM), lambda b, pt: (0, b, 0)),
                  pl.BlockSpec((T, 128), lambda b, pt: (b, 0)),
                  pl.BlockSpec((T, 128), lambda b, pt: (b, 0)),
                  pl.BlockSpec(lstrict.shape, lambda b, pt: (0, 0)),
                  pl.BlockSpec(memory_space=pl.ANY),
                  pl.BlockSpec(memory_space=pl.ANY)],
        out_specs=pl.BlockSpec((T, N_HEADS * HEAD_DIM), lambda b, pt: (b, 0)),
        scratch_shapes=[pltpu.VMEM((2, P, 128), F32),
                        pltpu.VMEM((2, n_pages, N_HEADS, page), F32),
                        pltpu.VMEM((n_pages, N_HEADS, page), F32),
                        pltpu.SemaphoreType.DMA((2,)), pltpu.SemaphoreType.DMA((2,))])
    return pl.pallas_call(
        functools.partial(_fox_sample_body, layer=layer, n_pages=n_pages, page=page, tk=tk, T=T),
        grid_spec=grid_spec,
        out_shape=jax.ShapeDtypeStruct((B * T, N_HEADS * HEAD_DIM), F32),
        compiler_params=_cp("arbitrary"),
        name="forget_sample",
    )(page_table, fq, fox_rows, small, lstrict, cache_kv, cache_lf_t)


def _mla_sample_body(pt_ref, q_ref, new_ref, wukt_ref, wuv_ref, gkn_ref, gkr_ref, c_hbm, o_ref, buf, inv_ref, sem,
                     *, layer, n_pages, page, ppt, T):
    pairs = [(lambda pg: c_hbm.at[layer, pg], lambda s, p: buf.at[s, p], sem)]
    slot = _gather_pipeline(pt_ref, n_pages, page, pairs)
    H = N_HEADS
    R = H * T
    tk = ppt * page
    scale = MLA_QK ** -0.5

    ql, qr = [], []
    for h in range(H):
        qh = q_ref[h].astype(F32)
        ql.append(_mm(qh[:, 0:MLA_NOPE] * gkn_ref[...], wukt_ref[h]))
        qr.append(qh[:, MLA_NOPE:MLA_QK] * gkr_ref[...])
    ql = jnp.concatenate(ql, axis=0)
    qr = jnp.concatenate(qr, axis=0)
    qcat = jnp.concatenate([ql, qr], axis=1).astype(MXU_DTYPE)
    wukt = wukt_ref[...].reshape(H * MLA_NOPE, KV_LORA)

    def key_scale(knt, kr2, n):
        ssq = jnp.sum((knt * knt).reshape(H, MLA_NOPE, n), axis=1) + kr2
        return lax.rsqrt(ssq / MLA_QK + EPS) * scale

    def tile(j):
        return jnp.concatenate([buf[slot, j * ppt + i] for i in range(ppt)], axis=1)

    def scale_step(j, carry):
        pt = tile(j)
        krt = pt[KV_LORA:KV_LORA + MLA_ROPE]
        knt = jnp.dot(wukt, pt[0:KV_LORA].astype(MXU_DTYPE), preferred_element_type=F32)
        inv_ref[j] = key_scale(knt, jnp.sum(krt * krt, axis=0, keepdims=True), tk)
        return carry

    lax.fori_loop(0, n_pages // ppt, scale_step, 0, unroll=2)

    def step(j, carry):
        m, l, acc = carry
        ptm = tile(j).astype(MXU_DTYPE)
        ckvt = ptm[0:KV_LORA]
        s = jnp.dot(qcat, ptm, preferred_element_type=F32)
        s = (s.reshape(H, T, tk) * inv_ref[j][:, None, :]).reshape(R, tk)
        return _softmax_update(s, None, m, l, acc, lambda w: _mm_nt(w, ckvt))

    init = (jnp.full((R, 1), NEG, F32), jnp.zeros((R, 1), F32), jnp.zeros((R, KV_LORA), F32))
    m, l, acc = lax.fori_loop(0, n_pages // ppt, step, init, unroll=2)

    rows = new_ref[...]
    ckv = rows[:, 0:KV_LORA]
    kr = rows[:, KV_LORA:KV_LORA + MLA_ROPE]
    inv = key_scale(_mm_nt(wukt, ckv), _mm3_nt_lhs_exact(jnp.ones((H, MLA_ROPE), MXU_DTYPE), kr * kr), T)
    s = (_mm_nt(qcat, rows).reshape(H, T, T) * inv[:, None, :]).reshape(R, T)
    mask = lax.broadcasted_iota(jnp.int32, (1, T), 1) <= _row_mod(R, T)
    m, l, acc = _softmax_update(s, mask, m, l, acc, ckv)
    lat = acc / jnp.maximum(l, 1e-30)
    for h in range(H):
        o_ref[:, h * HEAD_DIM:(h + 1) * HEAD_DIM] = _mm(lat[h * T:(h + 1) * T, :], wuv_ref[h])


def _mla_sample(page_table, mq, mla_rows, wukt, wuv, gkn, gkr, cache_t, layer, B, T):
    n_pages = page_table.shape[1]
    F, page = cache_t.shape[2], cache_t.shape[3]
    ppt = min(8, n_pages)
    assert n_pages % ppt == 0
    full = lambda a: pl.BlockSpec(a.shape, lambda b, pt: (0,) * a.ndim)
    grid_spec = pltpu.PrefetchScalarGridSpec(
        num_scalar_prefetch=1,
        grid=(B,),
        in_specs=[pl.BlockSpec((N_HEADS, T, MLA_QK), lambda b, pt: (0, b, 0)),
                  pl.BlockSpec((T, F), lambda b, pt: (b, 0)),
                  full(wukt), full(wuv), full(gkn), full(gkr),
                  pl.BlockSpec(memory_space=pl.ANY)],
        out_specs=pl.BlockSpec((T, N_HEADS * HEAD_DIM), lambda b, pt: (b, 0)),
        scratch_shapes=[pltpu.VMEM((2, n_pages, F, page), F32), pltpu.VMEM((n_pages // ppt, N_HEADS, ppt * page), F32),
                        pltpu.SemaphoreType.DMA((2,))])
    return pl.pallas_call(
        functools.partial(_mla_sample_body, layer=layer, n_pages=n_pages, page=page, ppt=ppt, T=T),
        grid_spec=grid_spec,
        out_shape=jax.ShapeDtypeStruct((B * T, N_HEADS * HEAD_DIM), F32),
        compiler_params=_cp("arbitrary"),
        name="latent_sample",
    )(page_table, mq, mla_rows, wukt, wuv, gkn, gkr, cache_t)


def _nsa_sample_body(pt_ref, nq_ref, nqr_ref, small_ref, newrows_ref, newwin_ref, pastwin_ref, pe_ref, w_ref, g_ref,
                     selw_ref, et_ref, c_hbm, o_ref, winout_ref, bufc, bufs, sem_c, sem_s, *, layer, n_pages, page, tk,
                     T, k_top):
    pairs = [(lambda pg: c_hbm.at[layer, pg, :, pl.ds(0, 128)], lambda s, p: bufc.at[s, pl.ds(p * page, page)], sem_c),
             (lambda pg: c_hbm.at[layer, pg, :, pl.ds(128, 128)], lambda s, p: bufs.at[s, pl.ds(p * page, page)], sem_s)]
    slot = _gather_pipeline(pt_ref, n_pages, page, pairs)
    R = N_HEADS * T
    P = n_pages * page
    nch = P // STRIDE_CMP
    nsp = selw_ref.shape[1]
    wb = pastwin_ref.shape[0]
    qidx = _row_mod(R, T)
    qpos = P + qidx
    qn = nq_ref[...].reshape(R, HEAD_DIM)
    qr = nqr_ref[...].reshape(R, HEAD_DIM)
    new_mask = lax.broadcasted_iota(jnp.int32, (1, T), 1) <= qidx
    init = (jnp.full((R, 1), NEG, F32), jnp.zeros((R, 1), F32), jnp.zeros((R, HEAD_DIM), F32))

    kcv = _compress(lambda l: bufc[slot, pl.ds(l, nch, stride=STRIDE_CMP), :], nch, pe_ref, w_ref, g_ref)
    cmp_end = lax.broadcasted_iota(jnp.int32, (1, nch), 1) * STRIDE_CMP + (L_CMP - 1)
    mask_c = cmp_end <= qpos
    s = jnp.where(mask_c, _mm_nt(qn, kcv[:, 0:HEAD_DIM]), NEG)
    p = jnp.where(mask_c, jnp.exp(s - jnp.max(s, axis=-1, keepdims=True)), 0.0)
    p = p / jnp.maximum(jnp.sum(p, axis=-1, keepdims=True), 1e-30)
    o_c = _mm(p, kcv[:, HEAD_DIM:])
    imp = _mm3(jnp.sum(p.reshape(N_HEADS, T, nch), axis=0), selw_ref[...])

    ids = lax.broadcasted_iota(jnp.int32, (1, nsp), 1)
    qp = P + lax.broadcasted_iota(jnp.int32, (T, 1), 0)
    cur = qp // L_SEL
    forced = (ids == 0) | (ids == cur) | (ids == cur - 1)
    score = jnp.where(forced, jnp.inf, jnp.where(ids * L_SEL <= qp, imp, -jnp.inf))
    selm = _topk_mask(score, -(-(P + T) // L_SEL), k_top).astype(MXU_DTYPE)
    selm = jnp.concatenate([selm] * N_HEADS, axis=0)

    def sel_step(j, carry):
        m, l, acc = carry
        k0 = pl.multiple_of(j * tk, tk)
        blk = _mm_nt(selm, et_ref[pl.ds(k0, tk), :]) > 0.5
        return _softmax_update(_mm_nt(qr, bufs[slot, pl.ds(k0, tk), 0:HEAD_DIM]), blk, m, l, acc,
                               bufs[slot, pl.ds(k0, tk), HEAD_DIM:])

    m, l, acc = lax.fori_loop(0, P // tk, sel_step, init, unroll=2)
    blk = (_mm_nt(selm, et_ref[pl.ds(P, T), :]) > 0.5) & new_mask
    m, l, acc = _softmax_update(_mm_nt(qr, newrows_ref[:, 128:192]), blk, m, l, acc, newrows_ref[:, 192:256])
    o_s = acc / jnp.maximum(l, 1e-30)

    mask_w = lax.broadcasted_iota(jnp.int32, (1, wb), 1) > qidx + (wb - WINDOW)
    m, l, acc = _softmax_update(_mm_nt(qr, pastwin_ref[:, 0:HEAD_DIM]), mask_w, *init, pastwin_ref[:, HEAD_DIM:])
    m, l, acc = _softmax_update(_mm_nt(qr, newwin_ref[:, 0:HEAD_DIM]), new_mask, m, l, acc, newwin_ref[:, HEAD_DIM:])
    o_w = acc / jnp.maximum(l, 1e-30)

    gates = [jnp.concatenate([small_ref[:, S_GATE + 3 * h + j:S_GATE + 3 * h + j + 1] for h in range(N_HEADS)], axis=0)
             for j in range(3)]
    _heads_to_lanes(o_ref, gates[0] * o_c + gates[1] * o_s + gates[2] * o_w, T, HEAD_DIM)

    keep = winout_ref.shape[0] - T
    winout_ref[0:keep, :] = pastwin_ref[wb - keep:wb, :]
    winout_ref[keep:keep + T, :] = newwin_ref[...]


def _nsa_sample(page_table, nq, nqr, small, nsa_rows, win_rows, state_win, pe, wblk, gk, selw, et, cache, layer, B, T):
    n_pages = page_table.shape[1]
    page = cache.shape[2]
    P = n_pages * page
    wb = state_win.shape[2]
    assert T < STRIDE_CMP and P % STRIDE_CMP == 0 and P + T >= WINDOW and wb >= WINDOW - T
    tk = min(2048, P)
    k_top = min(N_SELECT, -(-(P + T) // L_SEL))
    full = lambda a: pl.BlockSpec(a.shape, lambda b, pt: (0,) * a.ndim)
    head = pl.BlockSpec((N_HEADS, T, HEAD_DIM), lambda b, pt: (0, b, 0))
    grid_spec = pltpu.PrefetchScalarGridSpec(
        num_scalar_prefetch=1,
        grid=(B,),
        in_specs=[head, head,
                  pl.BlockSpec((T, 128), lambda b, pt: (b, 0)),
                  pl.BlockSpec((T, 256), lambda b, pt: (b, 0)),
                  pl.BlockSpec((T, 128), lambda b, pt: (b, 0)),
                  pl.BlockSpec((None, None, wb, 128), lambda b, pt: (layer, b, 0, 0)),
                  full(pe), full(wblk), full(gk), full(selw), full(et),
                  pl.BlockSpec(memory_space=pl.ANY)],
        out_specs=[pl.BlockSpec((T, N_HEADS * HEAD_DIM), lambda b, pt: (b, 0)),
                   pl.BlockSpec((None, WINDOW, 128), lambda b, pt: (b, 0, 0))],
        scratch_shapes=[pltpu.VMEM((2, P, 128), F32), pltpu.VMEM((2, P, 128), F32),
                        pltpu.SemaphoreType.DMA((2,)), pltpu.SemaphoreType.DMA((2,))])
    return pl.pallas_call(
        functools.partial(_nsa_sample_body, layer=layer, n_pages=n_pages, page=page, tk=tk, T=T, k_top=k_top),
        grid_spec=grid_spec,
        out_shape=[jax.ShapeDtypeStruct((B * T, N_HEADS * HEAD_DIM), F32),
                   jax.ShapeDtypeStruct((B, WINDOW, 128), F32)],
        compiler_params=_cp("arbitrary"),
        name="nsa_sample",
    )(page_table, nq, nqr, small, nsa_rows, win_rows, state_win, pe, wblk, gk, selw, et, cache)


def _block_ones(n, g):
    i = np.arange(n)
    return jnp.asarray(i[:, None] // g == i[None, :] // g, MXU_DTYPE)


def _sel_tables(S, nch):
    n_cmp = (S - L_CMP) // STRIDE_CMP + 1
    n_sel = -(-S // L_SEL)
    nsp = -(-n_sel // 128) * 128
    cs = np.arange(nch)[:, None] * STRIDE_CMP
    ss = np.arange(nsp)[None, :] * L_SEL
    w = np.maximum(np.minimum(cs + L_CMP, ss + L_SEL) - np.maximum(cs, ss), 0) // STRIDE_CMP
    w = np.where((np.arange(nch)[:, None] < n_cmp) & (np.arange(nsp)[None, :] < n_sel), w, 0)
    s_pad = -(-S // 16) * 16
    et = np.arange(s_pad)[:, None] // L_SEL == np.arange(nsp)[None, :]
    return jnp.asarray(w, MXU_DTYPE), jnp.asarray(et, MXU_DTYPE), n_sel


def _row_tile(M, pref):
    return pref if M % pref == 0 else M


def _key_tile(T):
    return next(t for t in (1024, 512, 256, 128, T) if T % t == 0)


def kernel(x_prompt, x_sample, cache_nsa, cache_fox_kv, cache_fox_logf, cache_mla, state_nsa_win, cache_mem, page_table, mem_prompt, norm_mix, w_in, b_fox, nsa_q_norm, nsa_k_norm, nsa_pe, nsa_w_cmp, fox_q_norm, fox_k_norm, mla_q_norm, mla_kv_norm, mla_w_uk, mla_w_uv, mla_k_norm, w_branch, w_o, norm_mem, mem_in_norm, w_q_mem, w_kv_mem, mem_q_norm, mem_k_norm, w_o_mem, norm_mlp, w_up, w_down):
    B, T, D = x_prompt.shape
    Bs, Ts, _ = x_sample.shape
    depth = w_in.shape[0]
    n_pages = page_table.shape[1]
    page = cache_nsa.shape[2]
    P = n_pages * page
    n_mem = mem_prompt.shape[1]
    bf = lambda a: a.astype(MXU_DTYPE)
    ones = lambda n: jnp.ones((depth, n), F32)

    splits = np.cumsum([0, 512, 384, 24, 512, 128, 8, 768, KV_LORA, MLA_ROPE, N_BRANCH * D])
    col = lambda i: w_in[:, :, splits[i]:splits[i + 1]]
    mla_q = col(6).reshape(depth, D, N_HEADS, MLA_QK)
    small_cols = jnp.concatenate([col(2), col(5), col(8), jnp.zeros((depth, D, 128 - 24 - 8 - MLA_ROPE), F32)], axis=2)
    w_rest = bf(jnp.concatenate([col(0), col(1), col(3), col(4),
                                 mla_q[..., :MLA_NOPE].reshape(depth, D, N_HEADS * MLA_NOPE),
                                 mla_q[..., MLA_NOPE:].reshape(depth, D, N_HEADS * MLA_ROPE),
                                 col(7), small_cols], axis=2))
    w_gate = bf(col(9))
    gqn = jnp.tile(nsa_q_norm, (1, N_HEADS))
    gkv = jnp.concatenate([ones(128), nsa_k_norm[:, 1], ones(64), nsa_k_norm[:, 2], ones(64)], axis=1)
    mkv = jnp.asarray(np.repeat([0, 0, 1, 0, 1, 0], HEAD_DIM)[None, :], F32)
    gqf = jnp.tile(fox_q_norm, (1, N_HEADS))
    gfk = jnp.concatenate([fox_k_norm, ones(64)], axis=1)
    gmn = jnp.tile(mla_q_norm[:, :MLA_NOPE], (1, N_HEADS))
    gmr = jnp.tile(mla_q_norm[:, MLA_NOPE:], (1, N_HEADS))
    bfox = jnp.concatenate([jnp.zeros((depth, S_LOGF), F32), b_fox, jnp.zeros((depth, 128 - S_LOGF - N_HEADS), F32)], axis=1)
    g64 = _block_ones(512, HEAD_DIM)
    g32 = _block_ones(256, MLA_ROPE)
    gx = jnp.asarray(np.arange(256)[:, None] // MLA_ROPE == np.arange(512)[None, :] // HEAD_DIM, MXU_DTYPE)
    gxt = gx.T
    pe = jnp.concatenate([nsa_pe[:, 0], nsa_pe[:, 1]], axis=2)
    zero = jnp.zeros_like(nsa_w_cmp[:, 0])
    wblk = bf(jnp.concatenate([jnp.concatenate([nsa_w_cmp[:, 0], zero], axis=3),
                               jnp.concatenate([zero, nsa_w_cmp[:, 1]], axis=3)], axis=2))
    gkc = jnp.concatenate([nsa_k_norm[:, 0], ones(64)], axis=1)
    wuk = bf(mla_w_uk.reshape(depth, KV_LORA, N_HEADS * MLA_NOPE))
    wuv = bf(mla_w_uv.reshape(depth, KV_LORA, N_HEADS * HEAD_DIM))
    wukt = bf(jnp.transpose(mla_w_uk, (0, 2, 3, 1)))
    wuvh = bf(jnp.transpose(mla_w_uv, (0, 2, 1, 3)))
    gkn = jnp.tile(mla_k_norm[:, :MLA_NOPE], (1, N_HEADS))
    wb_, wo_ = bf(w_branch), bf(w_o)
    wqm, wkvm, wom = bf(w_q_mem), bf(w_kv_mem), bf(w_o_mem)
    wu, wd = bf(w_up), bf(w_down)

    tm_post_p = _row_tile(T, 256)
    tm_post_s = _row_tile(Bs * Ts, 256)
    tab_p = _rope_tables(jnp.arange(T).astype(F32))
    tab_s = jnp.tile(_rope_tables((P + jnp.arange(Ts)).astype(F32)), (tm_post_s // Ts, 1))
    selw_p, et_p, _ = _sel_tables(T, T // STRIDE_CMP)
    selw_s, et_s, _ = _sel_tables(P + Ts, P // STRIDE_CMP)

    xp = x_prompt.reshape(B * T, D)
    xs = x_sample.reshape(Bs * Ts, D)
    tm_p = _row_tile(B * T, 512)
    tm_s = _row_tile(Bs * Ts, 512)
    tq = _row_tile(T, 128)
    tkf = _key_tile(T)
    cache_mla_t = jnp.swapaxes(cache_mla, 2, 3)
    cache_lf_t = jnp.swapaxes(cache_fox_logf, 2, 3)
    pr = np.arange(n_pages * N_HEADS)
    lstrict = jnp.asarray((pr[None, :] % N_HEADS == pr[:, None] % N_HEADS) & (pr[None, :] // N_HEADS < pr[:, None] // N_HEADS),
                          MXU_DTYPE)
    outs = [[] for _ in range(11)]
    r2 = lambda a: a[None, :]

    for l in range(depth):
        post_consts = [r2(gqn[l]), r2(gkv[l]), mkv, r2(gqf[l]), r2(gfk[l]), r2(gmn[l]), r2(gmr[l]),
                       r2(mla_kv_norm[l]), r2(bfox[l]), g64, gx, gxt, g32]
        cmp_consts = (pe[l], wblk[l], r2(gkc[l]))

        def mixers_in(x, tm, tab, Tg, tm_post):
            g = _norm_matmul(x, r2(norm_mix[l]), w_gate[l], tm)
            p = _norm_matmul(x, r2(norm_mix[l]), w_rest[l], tm)
            return (g,) + tuple(_post(p, tab, Tg, post_consts, tm_post))

        def tail(x, g, o_nsa, o_fox, o_mla, mem_kv, Bg, Tg, tm):
            x = _merge(x, o_nsa, o_fox, o_mla, g, wb_[l], wo_[l], tm)
            x = _mem_attn(x, mem_kv, r2(norm_mem[l]), wqm[l], r2(mem_q_norm[l]), wom[l], Bg, Tg, _row_tile(Tg, 256))
            return _mlp(x, r2(norm_mlp[l]), wu[l], wd[l], tm, min(1024, wu.shape[2]))

        mem_p = _mem_kv_rows(mem_prompt.reshape(B * n_mem, D), r2(mem_in_norm[l]), wkvm[l], r2(mem_k_norm[l]),
                             _row_tile(B * n_mem, 512)).reshape(B, n_mem, 2 * MEM_W)
        g, nq, nqr, fq, mq, nsa_rows, win_rows, fox_rows, mla_rows, small = mixers_in(xp, tm_p, tab_p, T, tm_post_p)
        kcv = _nsa_compress(nsa_rows, B, T, *cmp_consts)
        o_nsa = _nsa_prompt(nq, nqr, small, nsa_rows, win_rows, kcv, selw_p, et_p, B, T, tq)
        o_fox = _flash_prompt(fq, fox_rows, None, _fox_cumsum(small, B, T, tkf), B, T, tq, tkf, scale=1.0)
        kk, vv = _mla_kv(mla_rows, B, T, wuk[l], wuv[l], r2(gkn[l]), r2(mla_k_norm[l, MLA_NOPE:]), g64, tkf)
        o_mla = _flash_prompt(mq, kk, vv, None, B, T, tq, tkf, scale=MLA_QK ** -0.5)
        xp = tail(xp, g, o_nsa, o_fox, o_mla, mem_p, B, T, tm_p)
        wkeep = min(WINDOW, T)
        for i, a in enumerate((nsa_rows.reshape(B, T, -1), fox_rows.reshape(B, T, -1),
                               small[:, S_LOGF:S_LOGF + N_HEADS].reshape(B, T, -1), mla_rows.reshape(B, T, -1),
                               win_rows.reshape(B, T, -1)[:, T - wkeep:], mem_p)):
            outs[i].append(a)

        g, nq, nqr, fq, mq, nsa_rows, win_rows, fox_rows, mla_rows, small = mixers_in(xs, tm_s, tab_s, Ts, tm_post_s)
        o_nsa, new_win = _nsa_sample(page_table, nq, nqr, small, nsa_rows, win_rows, state_nsa_win, *cmp_consts,
                                     selw_s, et_s, cache_nsa, l, Bs, Ts)
        o_fox = _fox_sample(page_table, fq, fox_rows, small, lstrict, cache_fox_kv, cache_lf_t, l, Bs, Ts)
        o_mla = _mla_sample(page_table, mq, mla_rows, wukt[l], wuvh[l], r2(mla_k_norm[l, :MLA_NOPE]),
                            r2(mla_k_norm[l, MLA_NOPE:]), cache_mla_t, l, Bs, Ts)
        xs = tail(xs, g, o_nsa, o_fox, o_mla, cache_mem[l], Bs, Ts, tm_s)
        for i, a in enumerate((nsa_rows.reshape(Bs, Ts, -1), fox_rows.reshape(Bs, Ts, -1),
                               small[:, S_LOGF:S_LOGF + N_HEADS].reshape(Bs, Ts, -1), mla_rows.reshape(Bs, Ts, -1),
                               new_win)):
            outs[6 + i].append(a)

    return (xp.reshape(B, T, D), xs.reshape(Bs, Ts, D)) + tuple(jnp.stack(o) for o in outs)
```
